```python
import jax
import jax.numpy as jnp
from jax import lax
import numpy as np

D_MODEL = 1024
BATCH = 8
SEQ = 2048
DEPTH = 2

CTX_LEN = 256
GRID_W = 64
HEAD_DIM = 64
ROPE_THETA = 10000.0
NORM_EPS = 1e-6
MOD_CHUNKS = 6

GLA_HEADS = 4
GLA_DK = 32
GLA_DV = 64
GLA_GATE_RANK = 16
GLA_GATE_TAU = 16.0
GLA_CHUNK = 64
GLA_WIDTH = GLA_HEADS * GLA_DV

SWA_Q_HEADS = 6
SWA_KV_HEADS = 2
SWA_WINDOW = 128
SWA_BLOCK = 128
SWA_WIDTH = SWA_Q_HEADS * HEAD_DIM

GQA_Q_HEADS = 6
GQA_KV_HEADS = 2
GQA_BLOCK = 128
GQA_WIDTH = GQA_Q_HEADS * HEAD_DIM

MIX_WIDTH = GLA_WIDTH + SWA_WIDTH + GQA_WIDTH

IN_SIZES = (
    GLA_HEADS * GLA_DK,
    GLA_HEADS * GLA_DK,
    GLA_WIDTH,
    GLA_WIDTH,
    2 * GLA_GATE_RANK,
    SWA_WIDTH,
    SWA_KV_HEADS * HEAD_DIM,
    SWA_KV_HEADS * HEAD_DIM,
    GQA_WIDTH,
    GQA_KV_HEADS * HEAD_DIM,
    GQA_KV_HEADS * HEAD_DIM,
)
IN_TOTAL = sum(IN_SIZES)

FFN_DIM = 2816
FFN_CONV = 3

kernel_name = "hybrid_gla_swa_gqa_prefix_dit"


def rms_norm(x, w):
    xf = x.astype(jnp.float32)
    y = xf * lax.rsqrt(jnp.mean(xf * xf, axis=-1, keepdims=True) + NORM_EPS)
    return (y * w.astype(jnp.float32)).astype(x.dtype)


def to_heads(t, n_heads):
    return t.reshape(t.shape[0], t.shape[1], n_heads, -1)


def split_cols(p):
    out, start = [], 0
    for size in IN_SIZES:
        out.append(p[..., start:start + size])
        start += size
    return out


def rope_tables_2d(n_tokens):
    rows = n_tokens // GRID_W
    row = jnp.repeat(jnp.arange(rows), GRID_W).astype(jnp.float32)
    col = (jnp.arange(rows * GRID_W) % GRID_W).astype(jnp.float32)
    n_freq = HEAD_DIM // 4
    inv_freq = ROPE_THETA ** (-jnp.arange(n_freq, dtype=jnp.float32) / n_freq)
    ang_r = row[:, None] * inv_freq[None, :]
    ang_c = col[:, None] * inv_freq[None, :]
    ang = jnp.concatenate([ang_r, ang_r, ang_c, ang_c], axis=-1)
    return jnp.cos(ang), jnp.sin(ang)


def apply_rope_2d(x, cos, sin):
    xf = x.astype(jnp.float32)
    xs = xf.reshape(xf.shape[:-1] + (2, 2, HEAD_DIM // 4))
    rot = jnp.stack([-xs[..., 1, :], xs[..., 0, :]], axis=-2).reshape(xf.shape)
    return (xf * cos[:, None, :] + rot * sin[:, None, :]).astype(x.dtype)


def gla_chunked(q, k, v, log_a, s0, with_output):
    B, T, H, DK = q.shape
    C = GLA_CHUNK
    N = T // C

    def chunks(t):
        return jnp.moveaxis(t.astype(jnp.float32).reshape(B, N, C, H, t.shape[-1]), 1, 0)

    qc, kc, vc = chunks(q), chunks(k), chunks(v)
    bc = jnp.cumsum(chunks(log_a), axis=2)
    causal = jnp.tril(jnp.ones((C, C), dtype=bool))[None, :, :, None, None]

    def step(S, inp):
        qn, kn, vn, bn = inp
        b_last = bn[:, -1]
        S_next = jnp.exp(b_last)[..., None] * S + jnp.einsum(
            'bchk,bchv->bhkv', kn * jnp.exp(b_last[:, None] - bn), vn)
        if not with_output:
            return S_next, None
        o_inter = jnp.einsum('bchk,bhkv->bchv', qn * jnp.exp(bn), S)
        decay = jnp.exp(jnp.where(causal, bn[:, :, None] - bn[:, None, :], -jnp.inf))
        att = jnp.einsum('bthk,bshk,btshk->bhts', qn, kn, decay)
        o = o_inter + jnp.einsum('bhts,bshv->bthv', att, vn)
        return S_next, o

    s_final, o = lax.scan(step, s0, (qc, kc, vc, bc))
    if not with_output:
        return None, s_final
    o = jnp.moveaxis(o, 0, 1).reshape(B, T, H, v.shape[-1])
    return o, s_final


def gla_log_decay(z, w_gate, b_gate):
    g = (z @ w_gate + b_gate).astype(jnp.float32)
    return (jax.nn.log_sigmoid(g) / GLA_GATE_TAU).reshape(z.shape[0], z.shape[1], GLA_HEADS, GLA_DK)


def gla_mixer(lat, ctx, w_gate, b_gate, out_norm, need_ctx):
    def prep(parts):
        q, k, v, r, z = parts
        return (to_heads(q, GLA_HEADS) * (GLA_DK ** -0.5), to_heads(k, GLA_HEADS),
                to_heads(v, GLA_HEADS), to_heads(r, GLA_HEADS), z)

    ql, kl, vl, rl, zl = prep(lat)
    qc, kc, vc, rc, zc = prep(ctx)
    B = ql.shape[0]
    o_lat, o_ctx = 0.0, 0.0
    for direction in range(2):
        zs = slice(direction * GLA_GATE_RANK, (direction + 1) * GLA_GATE_RANK)
        seq_l = [ql, kl, vl, gla_log_decay(zl[..., zs], w_gate[direction], b_gate[direction])]
        seq_c = [qc, kc, vc, gla_log_decay(zc[..., zs], w_gate[direction], b_gate[direction])]
        if direction == 1:
            seq_l = [jnp.flip(t, axis=1) for t in seq_l]
            seq_c = [jnp.flip(t, axis=1) for t in seq_c]
        s0 = jnp.zeros((B, GLA_HEADS, GLA_DK, GLA_DV), jnp.float32)
        oc, s_ctx = gla_chunked(*seq_c, s0, need_ctx)
        ol, _ = gla_chunked(*seq_l, s_ctx, True)
        if direction == 1:
            ol = jnp.flip(ol, axis=1)
            if need_ctx:
                oc = jnp.flip(oc, axis=1)
        o_lat = o_lat + ol
        if need_ctx:
            o_ctx = o_ctx + oc

    def finish(o, r):
        y = rms_norm(o, out_norm) * jax.nn.silu(r.astype(jnp.float32))
        return y.astype(r.dtype).reshape(r.shape[0], r.shape[1], GLA_WIDTH)

    return finish(o_lat, rl), (finish(o_ctx, rc) if need_ctx else None)


def dense_gqa(q, k, v, sink=None):
    B, Tq, Hq, Dh = q.shape
    Hkv = k.shape[2]
    G = Hq // Hkv
    qg = q.reshape(B, Tq, Hkv, G, Dh)
    s = jnp.einsum('bqhgd,bkhd->bhgqk', qg, k).astype(jnp.float32) * (Dh ** -0.5)
    if sink is not None:
        s_sink = jnp.broadcast_to(sink.astype(jnp.float32).reshape(1, Hkv, G, 1, 1), s.shape[:-1] + (1,))
        p = jax.nn.softmax(jnp.concatenate([s, s_sink], axis=-1), axis=-1)[..., :-1]
    else:
        p = jax.nn.softmax(s, axis=-1)
    out = jnp.einsum('bhgqk,bkhd->bqhgd', p.astype(q.dtype), v)
    return out.reshape(B, Tq, Hq * Dh)


def swa_latent(q, k, v, k_ctx, v_ctx, sink):
    B, T, Hq, Dh = q.shape
    Hkv = k.shape[2]
    G = Hq // Hkv
    W = SWA_BLOCK
    nb = T // W
    L = k_ctx.shape[1]
    qb = q.reshape(B, nb, W, Hkv, G, Dh)

    def band(t):
        tp = jnp.pad(t, ((0, 0), (W, W), (0, 0), (0, 0))).reshape(B, nb + 2, W, Hkv, Dh)
        return jnp.concatenate([tp[:, :-2], tp[:, 1:-1], tp[:, 2:]], axis=2)

    kb, vb = band(k), band(v)
    blk = jnp.arange(nb)[:, None]
    qpos = blk * W + jnp.arange(W)[None, :]
    kpos = (blk - 1) * W + jnp.arange(3 * W)[None, :]
    valid = ((jnp.abs(qpos[:, :, None] - kpos[:, None, :]) <= SWA_WINDOW)
             & (kpos >= 0)[:, None, :] & (kpos < T)[:, None, :])
    scale = Dh ** -0.5
    s_loc = jnp.einsum('bnqhgd,bnkhd->bnhgqk', qb, kb).astype(jnp.float32) * scale
    s_loc = jnp.where(valid[None, :, None, None], s_loc, -jnp.inf)
    s_ctx = jnp.einsum('bnqhgd,bchd->bnhgqc', qb, k_ctx).astype(jnp.float32) * scale
    s_sink = jnp.broadcast_to(sink.astype(jnp.float32).reshape(1, 1, Hkv, G, 1, 1), s_loc.shape[:-1] + (1,))
    p = jax.nn.softmax(jnp.concatenate([s_loc, s_ctx, s_sink], axis=-1), axis=-1).astype(q.dtype)
    nk = 3 * W
    out = (jnp.einsum('bnhgqk,bnkhd->bnqhgd', p[..., :nk], vb)
           + jnp.einsum('bnhgqc,bchd->bnqhgd', p[..., nk:nk + L], v_ctx))
    return out.reshape(B, T, Hq * Dh)


def gqa_latent(q, k_all, v_all):
    B, T, Hq, Dh = q.shape
    nb = T // GQA_BLOCK
    qb = jnp.moveaxis(q.reshape(B, nb, GQA_BLOCK, Hq, Dh), 1, 0)
    out = lax.map(lambda qq: dense_gqa(qq, k_all, v_all), qb)
    return jnp.moveaxis(out, 0, 1).reshape(B, T, Hq * Dh)


def token_mixers(h_lat, h_ctx, w_in, gla_w_gate, gla_b_gate, gla_out_norm, swa_sink,
                 gqa_q_norm, gqa_k_norm, w_out, cos, sin, need_ctx):
    (a_q, a_k, a_v, a_r, a_z, b_q, b_k, b_v, c_q, c_k, c_v) = split_cols(h_lat @ w_in)
    (ca_q, ca_k, ca_v, ca_r, ca_z, cb_q, cb_k, cb_v, cc_q, cc_k, cc_v) = split_cols(h_ctx @ w_in)

    a_lat, a_ctx = gla_mixer((a_q, a_k, a_v, a_r, a_z), (ca_q, ca_k, ca_v, ca_r, ca_z),
                             gla_w_gate, gla_b_gate, gla_out_norm, need_ctx)

    bq_l = apply_rope_2d(to_heads(b_q, SWA_Q_HEADS), cos, sin)
    bk_l = apply_rope_2d(to_heads(b_k, SWA_KV_HEADS), cos, sin)
    bv_l = to_heads(b_v, SWA_KV_HEADS)
    bk_c, bv_c = to_heads(cb_k, SWA_KV_HEADS), to_heads(cb_v, SWA_KV_HEADS)
    b_lat = swa_latent(bq_l, bk_l, bv_l, bk_c, bv_c, swa_sink)

    cq_l = apply_rope_2d(rms_norm(to_heads(c_q, GQA_Q_HEADS), gqa_q_norm), cos, sin)
    ck_l = apply_rope_2d(rms_norm(to_heads(c_k, GQA_KV_HEADS), gqa_k_norm), cos, sin)
    cv_l = to_heads(c_v, GQA_KV_HEADS)
    ck_c = rms_norm(to_heads(cc_k, GQA_KV_HEADS), gqa_k_norm)
    cv_c = to_heads(cc_v, GQA_KV_HEADS)
    c_lat = gqa_latent(cq_l, jnp.concatenate([ck_l, ck_c], axis=1), jnp.concatenate([cv_l, cv_c], axis=1))

    out_lat = jnp.concatenate([a_lat, b_lat, c_lat], axis=-1) @ w_out
    if not need_ctx:
        return out_lat, None
    b_ctx = dense_gqa(to_heads(cb_q, SWA_Q_HEADS), bk_c, bv_c, swa_sink)
    c_ctx_out = dense_gqa(rms_norm(to_heads(cc_q, GQA_Q_HEADS), gqa_q_norm), ck_c, cv_c)
    out_ctx = jnp.concatenate([a_ctx, b_ctx, c_ctx_out], axis=-1) @ w_out
    return out_lat, out_ctx


def conv_ffn(h, w_up, conv_w, conv_b, w_down):
    u = h @ w_up
    u = lax.conv_general_dilated(u, conv_w[:, None, :], window_strides=(1,), padding=((1, 1),),
                                 dimension_numbers=('NWC', 'WIO', 'NWC'),
                                 feature_group_count=u.shape[-1]) + conv_b
    a, g = jnp.split(u, 2, axis=-1)
    return (jax.nn.silu(a) * g) @ w_down


def setup_inputs(seed: int = 0) -> dict:
    key = jax.random.key(seed)
    ks = jax.random.split(key, 24)
    f32 = jnp.float32
    D, L, F = D_MODEL, DEPTH, FFN_DIM

    def nrm(k, shape, scale):
        return jax.random.normal(k, shape, f32) * scale

    return {
        'x': nrm(ks[0], (BATCH, SEQ, D), 1.0),
        'c': nrm(ks[1], (BATCH, D), 1.0),
        'ctx': nrm(ks[2], (BATCH, CTX_LEN, D), 1.0),
        'c_ctx': nrm(ks[3], (D,), 1.0),
        'w_mod': nrm(ks[4], (L, D, MOD_CHUNKS * D), 0.5 * D ** -0.5),
        'b_mod': nrm(ks[5], (L, MOD_CHUNKS * D), 0.01),
        'attn_pre_norm': 1.0 + nrm(ks[6], (L, D), 0.05),
        'attn_post_norm': 1.0 + nrm(ks[7], (L, D), 0.05),
        'ffn_pre_norm': 1.0 + nrm(ks[8], (L, D), 0.05),
        'ffn_post_norm': 1.0 + nrm(ks[9], (L, D), 0.05),
        'w_in': nrm(ks[10], (L, D, IN_TOTAL), D ** -0.5),
        'gla_w_gate': nrm(ks[11], (L, 2, GLA_GATE_RANK, GLA_HEADS * GLA_DK), GLA_GATE_RANK ** -0.5),
        'gla_b_gate': nrm(ks[12], (L, 2, GLA_HEADS * GLA_DK), 0.1),
        'gla_out_norm': 1.0 + nrm(ks[13], (L, GLA_DV), 0.05),
        'swa_sink': nrm(ks[14], (L, SWA_Q_HEADS), 0.5),
        'gqa_q_norm': 1.0 + nrm(ks[15], (L, HEAD_DIM), 0.05),
        'gqa_k_norm': 1.0 + nrm(ks[16], (L, HEAD_DIM), 0.05),
        'w_out': nrm(ks[17], (L, MIX_WIDTH, D), MIX_WIDTH ** -0.5),
        'ffn_w_up': nrm(ks[18], (L, D, 2 * F), D ** -0.5),
        'ffn_conv_w': nrm(ks[19], (L, FFN_CONV, 2 * F), FFN_CONV ** -0.5),
        'ffn_conv_b': nrm(ks[20], (L, 2 * F), 0.01),
        'ffn_w_down': nrm(ks[21], (L, F, D), F ** -0.5),
    }


def reference(x, c, ctx, c_ctx, w_mod, b_mod, attn_pre_norm, attn_post_norm, ffn_pre_norm,
              ffn_post_norm, w_in, gla_w_gate, gla_b_gate, gla_out_norm, swa_sink, gqa_q_norm,
              gqa_k_norm, w_out, ffn_w_up, ffn_conv_w, ffn_conv_b, ffn_w_down):
    cos, sin = rope_tables_2d(x.shape[1])
    for layer in range(DEPTH):
        need_ctx = layer < DEPTH - 1
        mod_lat = jnp.split(jax.nn.silu(c) @ w_mod[layer] + b_mod[layer], MOD_CHUNKS, axis=-1)
        mod_ctx = jnp.split(jax.nn.silu(c_ctx) @ w_mod[layer] + b_mod[layer], MOD_CHUNKS, axis=-1)
        sh1, sc1, g1, sh2, sc2, g2 = [m[:, None, :] for m in mod_lat]
        csh1, csc1, cg1, csh2, csc2, cg2 = mod_ctx

        h_lat = rms_norm(x, attn_pre_norm[layer]) * (1 + sc1) + sh1
        h_ctx = rms_norm(ctx, attn_pre_norm[layer]) * (1 + csc1) + csh1
        mix_lat, mix_ctx = token_mixers(h_lat, h_ctx, w_in[layer], gla_w_gate[layer], gla_b_gate[layer],
                                        gla_out_norm[layer], swa_sink[layer], gqa_q_norm[layer],
                                        gqa_k_norm[layer], w_out[layer], cos, sin, need_ctx)
        x = x + g1 * rms_norm(mix_lat, attn_post_norm[layer])

        h_lat = rms_norm(x, ffn_pre_norm[layer]) * (1 + sc2) + sh2
        f_lat = conv_ffn(h_lat, ffn_w_up[layer], ffn_conv_w[layer], ffn_conv_b[layer], ffn_w_down[layer])
        x = x + g2 * rms_norm(f_lat, ffn_post_norm[layer])

        if need_ctx:
            ctx = ctx + cg1 * rms_norm(mix_ctx, attn_post_norm[layer])
            h_c = rms_norm(ctx, ffn_pre_norm[layer]) * (1 + csc2) + csh2
            f_ctx = conv_ffn(h_c, ffn_w_up[layer], ffn_conv_w[layer], ffn_conv_b[layer], ffn_w_down[layer])
            ctx = ctx + cg2 * rms_norm(f_ctx, ffn_post_norm[layer])
    return x
```

```python
import functools

import jax
import jax.numpy as jnp
from jax import lax
from jax.experimental import pallas as pl
from jax.experimental.pallas import tpu as pltpu

F32 = jnp.float32
BF16 = jnp.bfloat16

D_MODEL = 1024
BATCH = 8
SEQ = 2048
DEPTH = 2
CTX_LEN = 256
TOK = CTX_LEN + SEQ
GRID_W = 64
HEAD_DIM = 64
ROPE_THETA = 10000.0
NORM_EPS = 1e-6
MOD_CHUNKS = 6

GLA_HEADS = 4
GLA_DK = 32
GLA_DV = 64
GLA_GATE_RANK = 16
GLA_GATE_TAU = 16.0
GLA_QK = GLA_HEADS * GLA_DK
GLA_WIDTH = GLA_HEADS * GLA_DV

Q_HEADS = 6
KV_HEADS = 2
GROUP = Q_HEADS // KV_HEADS
Q_WIDTH = Q_HEADS * HEAD_DIM
KV_WIDTH = KV_HEADS * HEAD_DIM
SWA_BLOCK = 128

FFN_DIM = 2816
FFN_CHUNK = 256

LANES = 128
TILE = 256
N_TILES = TOK // TILE
HALO = 16
GLA_BLOCK = 256
GLA_CHUNK = 64
GLA_SUB = 16
ATT_TQ = 256
VMEM_LIMIT = 56 * 1024 * 1024

C_GQ, C_GK, C_GV, C_GR = 0, 128, 256, 512
C_BQ, C_BK, C_BV = 768, 1152, 1280
C_CQ, C_CK, C_CV = 1408, 1792, 1920
C_GZ = 2048
IN_COLS = 2176
HEAD_PERM = (0, 3, 1, 4, 2, 5)

_NT = (((1,), (1,)), ((), ()))


def _dot(a, b):
    return jnp.dot(a, b, preferred_element_type=F32)


def _dot_nt(a, b):
    return lax.dot_general(a, b, _NT, preferred_element_type=F32)


def _idiv(x, n):
    assert n & (n - 1) == 0
    return lax.shift_right_logical(x, jnp.int32(n.bit_length() - 1))


def _split2(x):
    hi = x.astype(BF16)
    lo = (x - hi.astype(F32)).astype(BF16)
    return hi, lo


def _split3(x):
    h1 = x.astype(BF16)
    r1 = x - h1.astype(F32)
    h2 = r1.astype(BF16)
    h3 = (r1 - h2.astype(F32)).astype(BF16)
    return h1, h2, h3


def _seg_ones(width, seg):
    r = _idiv(lax.broadcasted_iota(jnp.int32, (width, width), 0), seg)
    c = _idiv(lax.broadcasted_iota(jnp.int32, (width, width), 1), seg)
    return jnp.where(r == c, 1.0, 0.0).astype(BF16)


def _seg_sum(x, ones):
    hi, lo = _split2(x)
    return _dot(hi, ones) + _dot(lo, ones)


def _row_rms(x, w):
    ms = jnp.mean(x * x, axis=-1, keepdims=True)
    return x * lax.rsqrt(ms + NORM_EPS) * w


def _head_rms(x, ones, w):
    ss = _seg_sum(x * x, ones)
    return x * lax.rsqrt(ss * (1.0 / HEAD_DIM) + NORM_EPS) * w


def _silu(x):
    return x / (1.0 + jnp.exp(-x))


def _rope(x, cos, sin_up, sin_dn):
    return x * cos + pltpu.roll(x, LANES - 16, 1) * sin_up + pltpu.roll(x, 16, 1) * sin_dn


def _mod_kernel(c_ref, w_ref, b_ref, o_ref):
    a = _silu(c_ref[...]).astype(BF16)
    o_ref[0] = _dot(a, w_ref[0].astype(BF16)) + b_ref[0]


def _modulation(cs, w_mod, b_mod):
    tn = 1536
    n = MOD_CHUNKS * D_MODEL
    return pl.pallas_call(
        _mod_kernel,
        grid=(DEPTH, n // tn),
        in_specs=[
            pl.BlockSpec((16, D_MODEL), lambda l, j: (0, 0)),
            pl.BlockSpec((1, D_MODEL, tn), lambda l, j: (l, 0, j)),
            pl.BlockSpec((1, 1, tn), lambda l, j: (l, 0, j)),
        ],
        out_specs=pl.BlockSpec((1, 16, tn), lambda l, j: (l, 0, j)),
        out_shape=jax.ShapeDtypeStruct((DEPTH, 16, n), F32),
        compiler_params=pltpu.CompilerParams(
            dimension_semantics=("arbitrary", "arbitrary"), vmem_limit_bytes=VMEM_LIMIT),
        name="modulation",
    )(cs, w_mod, b_mod.reshape(DEPTH, 1, n))


def _mod_spec(layer, t0):
    return pl.BlockSpec(
        (1, 1, MOD_CHUNKS, D_MODEL),
        lambda b, i: (layer, jnp.where(i + t0 == 0, BATCH, b), 0, 0))


def _in_proj_kernel(x_ref, mod_ref, nw_ref, w_ref, cos_ref, su_ref, sd_ref, qn_ref, kn_ref,
                    gq_ref, gk_ref, gv_ref, gr_ref, gz_ref,
                    bq_ref, bk_ref, bv_ref, cq_ref, ck_ref, cv_ref):
    x = x_ref[0]
    sh = mod_ref[0, 0, 0:1, :]
    sc = mod_ref[0, 0, 1:2, :]
    h = (_row_rms(x, nw_ref[...]) * (1.0 + sc) + sh).astype(BF16)

    def proj(c0, width):
        return _dot(h, w_ref[:, c0:c0 + width])

    cos, su, sd = cos_ref[...], su_ref[...], sd_ref[...]
    ones = _seg_ones(LANES, HEAD_DIM)
    att_scale = HEAD_DIM ** -0.5

    gq_ref[0] = proj(C_GQ, GLA_QK) * (GLA_DK ** -0.5)
    gk_ref[0] = proj(C_GK, GLA_QK)
    gv_ref[0] = proj(C_GV, GLA_WIDTH)
    gr_ref[0] = proj(C_GR, GLA_WIDTH)
    gz_ref[0] = proj(C_GZ, LANES)

    for g in range(GROUP):
        lo = g * LANES
        bq = proj(C_BQ + lo, LANES)
        bq_ref[0, :, lo:lo + LANES] = (_rope(bq, cos, su, sd) * att_scale).astype(BF16)
        cq = _head_rms(proj(C_CQ + lo, LANES), ones, qn_ref[...])
        cq_ref[0, :, lo:lo + LANES] = (_rope(cq, cos, su, sd) * att_scale).astype(BF16)
    bk_ref[0] = _rope(proj(C_BK, KV_WIDTH), cos, su, sd).astype(BF16)
    bv_ref[0] = proj(C_BV, KV_WIDTH).astype(BF16)
    ck = _head_rms(proj(C_CK, KV_WIDTH), ones, kn_ref[...])
    ck_ref[0] = _rope(ck, cos, su, sd).astype(BF16)
    cv_ref[0] = proj(C_CV, KV_WIDTH).astype(BF16)


def _in_proj(x_all, mod, layer, nw, w, cos, su, sd, qn, kn):
    def tok(width):
        return pl.BlockSpec((1, TILE, width), lambda b, i: (b, i, 0))

    def const(shape):
        return pl.BlockSpec(shape, lambda b, i: (0,) * len(shape))

    def tab():
        return pl.BlockSpec((TILE, LANES), lambda b, i: (i, 0))

    widths = [(GLA_QK, F32), (GLA_QK, F32), (GLA_WIDTH, F32), (GLA_WIDTH, F32), (LANES, F32),
              (Q_WIDTH, BF16), (KV_WIDTH, BF16), (KV_WIDTH, BF16),
              (Q_WIDTH, BF16), (KV_WIDTH, BF16), (KV_WIDTH, BF16)]
    return pl.pallas_call(
        _in_proj_kernel,
        grid=(BATCH, N_TILES),
        in_specs=[tok(D_MODEL), _mod_spec(layer, 0), const((1, D_MODEL)), const((D_MODEL, IN_COLS)),
                  tab(), tab(), tab(), const((1, LANES)), const((1, LANES))],
        out_specs=[tok(wd) for wd, _ in widths],
        out_shape=[jax.ShapeDtypeStruct((BATCH, TOK, wd), dt) for wd, dt in widths],
        compiler_params=pltpu.CompilerParams(
            dimension_semantics=("parallel", "parallel"), vmem_limit_bytes=VMEM_LIMIT),
        name="in_proj",
    )(x_all, mod, nw, w, cos, su, sd, qn, kn)


def _lane_group_mask(width, group, h):
    lane = lax.broadcasted_iota(jnp.int32, (1, width), 1)
    return _idiv(lane, group) == h


def _gla_direction(q_ref, k_ref, v_ref, z_ref, wg, bg, wst_ref, o_ref, st_ref, b_s, e_s, rev):
    nb, c_len, sb = GLA_BLOCK, GLA_CHUNK, GLA_SUB
    n_chunks, n_sub = nb // c_len, c_len // sb

    g = _dot(z_ref[0].astype(BF16), wg) + bg
    la = (jnp.minimum(g, 0.0) - jnp.log1p(jnp.exp(-jnp.abs(g)))) * (1.0 / GLA_GATE_TAU)

    row = lax.broadcasted_iota(jnp.int32, (nb, nb), 0)
    col = lax.broadcasted_iota(jnp.int32, (nb, nb), 1)
    inside = _idiv(row, c_len) == _idiv(col, c_len)
    tri = jnp.where(inside & ((col >= row) if rev else (col <= row)), 1.0, 0.0).astype(BF16)
    h1, h2, h3 = _split3(la)
    b = _dot(tri, h1) + _dot(tri, h2) + _dot(tri, h3)
    b_s[...] = b

    q, k, v = q_ref[0], k_ref[0], v_ref[0]

    t_loc = lax.broadcasted_iota(jnp.int32, (sb, 1), 0)

    def diag_block(jb, carry):
        r0 = pl.multiple_of(jb * sb, sb)
        bb = b_s[pl.ds(r0, sb), :]
        qb = q_ref[0, pl.ds(r0, sb), :]
        for s in range(sb):
            bs = b_s[pl.ds(r0 + s, 1), :]
            ks = k_ref[0, pl.ds(r0 + s, 1), :]
            e = qb * ks * jnp.exp(jnp.minimum(bb - bs, 0.0))
            keep = (t_loc <= s) if rev else (t_loc >= s)
            e_s[pl.ds(r0, sb), s * LANES:(s + 1) * LANES] = jnp.where(keep, e, 0.0).astype(BF16)
        return carry

    lax.fori_loop(0, nb // sb, diag_block, 0)
    a_diag = _dot(e_s[...], wst_ref[...])

    hm_k = [_lane_group_mask(GLA_QK, GLA_DK, h) for h in range(GLA_HEADS)]
    hm_v = [_lane_group_mask(GLA_WIDTH, GLA_DV, h) for h in range(GLA_HEADS)]
    st_row = _idiv(lax.broadcasted_iota(jnp.int32, (GLA_WIDTH, GLA_QK), 0), GLA_DV)
    st_col = _idiv(lax.broadcasted_iota(jnp.int32, (GLA_WIDTH, GLA_QK), 1), GLA_DK)
    st_mask = st_row == st_col
    v_t = jnp.transpose(v).astype(BF16)
    tok_row = _idiv(lax.broadcasted_iota(jnp.int32, (nb, 1), 0), c_len)

    q_hat, kv_t, d_row, o_intra = [], [], [], []
    for c in range(n_chunks):
        cs = slice(c * c_len, (c + 1) * c_len)
        bc, qc, kc, vc = b[cs], q[cs], k[cs], v[cs]
        end = 0 if rev else c_len - 1
        b_end = bc[end:end + 1]
        q_hat.append((qc * jnp.exp(bc)).astype(BF16))
        d_row.append(jnp.exp(b_end))
        k_hat = jnp.where(tok_row == c, k * jnp.exp(jnp.minimum(b_end - b, 0.0)), 0.0)
        kv_t.append(jnp.where(st_mask, _dot(v_t, k_hat.astype(BF16)), 0.0))

        blocks = [jnp.zeros((sb, GLA_WIDTH), F32) for _ in range(n_sub)]
        for j in range(n_sub):
            rb = slice(j * sb, (j + 1) * sb)
            v_exp = jnp.concatenate(
                [jnp.where(hm_v[h], vc[rb], 0.0) for h in range(GLA_HEADS)], axis=0).astype(BF16)
            ad = a_diag[c * c_len + j * sb:c * c_len + (j + 1) * sb]
            has_later = (j > 0) if rev else (j < n_sub - 1)
            if has_later:
                edge = j * sb if rev else (j + 1) * sb - 1
                ref_b = bc[edge:edge + 1]
                later = slice(0, j * sb) if rev else slice((j + 1) * sb, c_len)
                q_l = (qc[later] * jnp.exp(bc[later] - ref_b)).astype(BF16)
                k_j = kc[rb] * jnp.exp(ref_b - bc[rb])
                k_tile = jnp.concatenate(
                    [jnp.where(hm_k[h], k_j, 0.0) for h in range(GLA_HEADS)], axis=0).astype(BF16)
                sc = _dot_nt(q_l, k_tile)
                p = jnp.concatenate([sc, ad], axis=0) if rev else jnp.concatenate([ad, sc], axis=0)
                first = 0 if rev else j
            else:
                p, first = ad, j
            contrib = _dot(p.astype(BF16), v_exp)
            for r in range(contrib.shape[0] // sb):
                blocks[first + r] = blocks[first + r] + contrib[r * sb:(r + 1) * sb]
        o_intra.append(jnp.concatenate(blocks, axis=0))

    state = st_ref[...]
    outs = [None] * n_chunks
    for c in (reversed(range(n_chunks)) if rev else range(n_chunks)):
        outs[c] = o_intra[c] + _dot_nt(q_hat[c], state.astype(BF16))
        state = state * d_row[c] + kv_t[c]
    st_ref[...] = state
    o_ref[0, 0] = jnp.concatenate(outs, axis=0)


def _gla_kernel(qf, kf, vf, zf, qb, kb, vb, zb, wg_ref, bg_ref, wst_ref, of_ref, ob_ref,
                sf_ref, sb_ref, bf_s, bb_s, ef_s, eb_s):
    @pl.when(pl.program_id(1) == 0)
    def _():
        sf_ref[...] = jnp.zeros_like(sf_ref)
        sb_ref[...] = jnp.zeros_like(sb_ref)

    _gla_direction(qf, kf, vf, zf, wg_ref[0], bg_ref[0], wst_ref, of_ref, sf_ref, bf_s, ef_s, False)
    _gla_direction(qb, kb, vb, zb, wg_ref[1], bg_ref[1], wst_ref, ob_ref, sb_ref, bb_s, eb_s, True)


def _gla(gq, gk, gv, gz, wg, bg, wst):
    nblk = TOK // GLA_BLOCK

    def fwd(b, s):
        return (b, s, 0)

    def bwd(b, s):
        return (b, jnp.where(s == 0, 0, nblk - s), 0)

    def specs(index_map):
        return [pl.BlockSpec((1, GLA_BLOCK, wd), index_map)
                for wd in (GLA_QK, GLA_QK, GLA_WIDTH, LANES)]

    def const(shape):
        return pl.BlockSpec(shape, lambda b, s: (0,) * len(shape))

    return pl.pallas_call(
        _gla_kernel,
        grid=(BATCH, nblk),
        in_specs=specs(fwd) + specs(bwd) + [const((2, LANES, GLA_QK)), const((2, 1, GLA_QK)),
                                            const((GLA_SUB * LANES, GLA_HEADS * GLA_SUB))],
        out_specs=[pl.BlockSpec((1, 1, GLA_BLOCK, GLA_WIDTH), lambda b, s: (0,) + fwd(b, s)),
                   pl.BlockSpec((1, 1, GLA_BLOCK, GLA_WIDTH), lambda b, s: (0,) + bwd(b, s))],
        out_shape=[jax.ShapeDtypeStruct((1, BATCH, TOK, GLA_WIDTH), F32)] * 2,
        scratch_shapes=[pltpu.VMEM((GLA_WIDTH, GLA_QK), F32), pltpu.VMEM((GLA_WIDTH, GLA_QK), F32),
                        pltpu.VMEM((GLA_BLOCK, GLA_QK), F32), pltpu.VMEM((GLA_BLOCK, GLA_QK), F32),
                        pltpu.VMEM((GLA_BLOCK, GLA_SUB * LANES), BF16),
                        pltpu.VMEM((GLA_BLOCK, GLA_SUB * LANES), BF16)],
        compiler_params=pltpu.CompilerParams(
            dimension_semantics=("parallel", "arbitrary"), vmem_limit_bytes=VMEM_LIMIT),
        name="gla",
    )(gq, gk, gv, gz, gq, gk, gv, gz, wg, bg, wst)


def _attend(q_ref, keys, values, sink, valid):
    tq = q_ref.shape[1]
    lane = lax.broadcasted_iota(jnp.int32, (1, LANES), 1)
    outs = []
    for h in range(KV_HEADS):
        mine = _idiv(lane, HEAD_DIM) == h
        qz = jnp.concatenate(
            [jnp.where(mine, q_ref[0, :, g * LANES:(g + 1) * LANES], 0) for g in range(GROUP)],
            axis=0)
        s = _dot_nt(qz, keys)
        if valid is not None:
            s = jnp.where(valid, s, -1e30)
        m = jnp.max(s, axis=-1, keepdims=True)
        if sink is not None:
            m = jnp.maximum(m, sink[h])
        p = jnp.exp(s - m)
        den = jnp.sum(p, axis=-1, keepdims=True)
        if sink is not None:
            den = den + jnp.exp(sink[h] - m)
        outs.append(_dot(p.astype(BF16), values) / den)
    first = lane < HEAD_DIM
    return jnp.concatenate(
        [jnp.where(first, outs[0][g * tq:(g + 1) * tq], outs[1][g * tq:(g + 1) * tq])
         for g in range(GROUP)], axis=1)


def _sink_columns(sink_ref, tq):
    g_row = _idiv(lax.broadcasted_iota(jnp.int32, (GROUP * tq, 1), 0), tq)
    cols = []
    for h in range(KV_HEADS):
        c = jnp.full((GROUP * tq, 1), sink_ref[h, GROUP - 1], F32)
        for g in range(GROUP - 1):
            c = jnp.where(g_row == g, sink_ref[h, g], c)
        cols.append(c)
    return cols


def _swa_kernel(sink_ref, q_ref, kc_ref, vc_ref, kp_ref, kk_ref, kn_ref, vp_ref, vk_ref, vn_ref,
                o_ref, *, ctx_blocks):
    i = pl.program_id(1)
    w = SWA_BLOCK
    sink = _sink_columns(sink_ref, w)

    @pl.when(i < ctx_blocks)
    def _():
        o_ref[0] = _attend(q_ref, kc_ref[0], vc_ref[0], sink, None).astype(o_ref.dtype)

    @pl.when(i >= ctx_blocks)
    def _():
        li = i - ctx_blocks
        keys = jnp.concatenate([kp_ref[0], kk_ref[0], kn_ref[0], kc_ref[0]], axis=0)
        values = jnp.concatenate([vp_ref[0], vk_ref[0], vn_ref[0], vc_ref[0]], axis=0)
        nk = 3 * w + CTX_LEN
        t = lax.broadcasted_iota(jnp.int32, (GROUP * w, nk), 0) & (w - 1)
        kk = lax.broadcasted_iota(jnp.int32, (GROUP * w, nk), 1)
        kpos = (li - 1) * w + kk
        valid = ((kk - t >= 0) & (kk - t <= 2 * w) & (kpos >= 0) & (kpos < SEQ)) | (kk >= 3 * w)
        o_ref[0] = _attend(q_ref, keys, values, sink, valid).astype(o_ref.dtype)


def _swa(sink, bq, bk, bv, with_ctx):
    w = SWA_BLOCK
    ctx_blocks = CTX_LEN // w
    n_lat = SEQ // w
    q_off = 0 if with_ctx else ctx_blocks
    n_steps = n_lat + (ctx_blocks if with_ctx else 0)
    cb = ctx_blocks if with_ctx else 0

    def lat_block(shift):
        def index_map(b, i):
            li = jnp.clip(i - cb + shift, 0, n_lat - 1)
            return (b, li + ctx_blocks, 0)
        return pl.BlockSpec((1, w, KV_WIDTH), index_map)

    ctx_spec = pl.BlockSpec((1, CTX_LEN, KV_WIDTH), lambda b, i: (b, 0, 0))
    return pl.pallas_call(
        functools.partial(_swa_kernel, ctx_blocks=cb),
        grid=(BATCH, n_steps),
        in_specs=[pl.BlockSpec(memory_space=pltpu.SMEM),
                  pl.BlockSpec((1, w, Q_WIDTH), lambda b, i: (b, i + q_off, 0)),
                  ctx_spec, ctx_spec,
                  lat_block(-1), lat_block(0), lat_block(1),
                  lat_block(-1), lat_block(0), lat_block(1)],
        out_specs=pl.BlockSpec((1, w, Q_WIDTH), lambda b, i: (b, i, 0)),
        out_shape=jax.ShapeDtypeStruct((BATCH, n_steps * w, Q_WIDTH), BF16),
        compiler_params=pltpu.CompilerParams(
            dimension_semantics=("parallel", "parallel"), vmem_limit_bytes=VMEM_LIMIT),
        name="swa",
    )(sink, bq, bk, bv, bk, bk, bk, bv, bv, bv)


def _gqa_kernel(q_ref, k_ref, v_ref, o_ref, *, ctx_blocks):
    i = pl.program_id(1)

    @pl.when(i < ctx_blocks)
    def _():
        o_ref[0] = _attend(q_ref, k_ref[0, 0:CTX_LEN, :], v_ref[0, 0:CTX_LEN, :],
                           None, None).astype(o_ref.dtype)

    @pl.when(i >= ctx_blocks)
    def _():
        o_ref[0] = _attend(q_ref, k_ref[0], v_ref[0], None, None).astype(o_ref.dtype)


def _gqa(cq, ck, cv, with_ctx):
    tq = ATT_TQ
    ctx_blocks = CTX_LEN // tq
    q_off = 0 if with_ctx else ctx_blocks
    n_steps = SEQ // tq + (ctx_blocks if with_ctx else 0)
    kv_spec = pl.BlockSpec((1, TOK, KV_WIDTH), lambda b, i: (b, 0, 0))
    return pl.pallas_call(
        functools.partial(_gqa_kernel, ctx_blocks=ctx_blocks if with_ctx else 0),
        grid=(BATCH, n_steps),
        in_specs=[pl.BlockSpec((1, tq, Q_WIDTH), lambda b, i: (b, i + q_off, 0)), kv_spec, kv_spec],
        out_specs=pl.BlockSpec((1, tq, Q_WIDTH), lambda b, i: (b, i, 0)),
        out_shape=jax.ShapeDtypeStruct((BATCH, n_steps * tq, Q_WIDTH), BF16),
        compiler_params=pltpu.CompilerParams(
            dimension_semantics=("parallel", "parallel"), vmem_limit_bytes=VMEM_LIMIT),
        name="gqa",
    )(cq, ck, cv)


def _out_proj_kernel(x_ref, of_ref, ob_ref, r_ref, b_ref, c_ref, mod_ref, w_ref, on_ref,
                     post_ref, pre_ref, x1_ref, h2_ref):
    ones = _seg_ones(LANES, GLA_DV)
    mix = None
    for part in range(GLA_WIDTH // LANES):
        ls = slice(part * LANES, (part + 1) * LANES)
        o = of_ref[0, 0, :, ls] + ob_ref[0, 0, :, ls]
        ya = _head_rms(o, ones, on_ref[...]) * _silu(r_ref[0, :, ls])
        term = _dot(ya.astype(BF16), w_ref[ls, :])
        mix = term if mix is None else mix + term
    mix = mix + _dot(b_ref[0], w_ref[GLA_WIDTH:GLA_WIDTH + Q_WIDTH, :])
    mix = mix + _dot(c_ref[0], w_ref[GLA_WIDTH + Q_WIDTH:, :])
    g1 = mod_ref[0, 0, 2:3, :]
    sh2 = mod_ref[0, 0, 3:4, :]
    sc2 = mod_ref[0, 0, 4:5, :]
    x1 = x_ref[0] + g1 * _row_rms(mix, post_ref[...])
    x1_ref[0] = x1
    h2_ref[0] = (_row_rms(x1, pre_ref[...]) * (1.0 + sc2) + sh2).astype(BF16)


def _out_proj(x_all, o_f, o_b, gr, b_att, c_att, mod, layer, w, on, post, pre, with_ctx):
    t0 = 0 if with_ctx else 1
    n_tiles = N_TILES - t0

    def tok(width, off):
        return pl.BlockSpec((1, TILE, width), lambda b, i: (b, i + off, 0))

    def const(shape):
        return pl.BlockSpec(shape, lambda b, i: (0,) * len(shape))

    gla_spec = pl.BlockSpec((1, 1, TILE, GLA_WIDTH), lambda b, i: (0, b, i + t0, 0))
    return pl.pallas_call(
        _out_proj_kernel,
        grid=(BATCH, n_tiles),
        in_specs=[tok(D_MODEL, t0), gla_spec, gla_spec, tok(GLA_WIDTH, t0),
                  tok(Q_WIDTH, 0), tok(Q_WIDTH, 0), _mod_spec(layer, t0),
                  const((D_MODEL, D_MODEL)), const((1, LANES)), const((1, D_MODEL)),
                  const((1, D_MODEL))],
        out_specs=[tok(D_MODEL, 0), tok(D_MODEL, 0)],
        out_shape=[jax.ShapeDtypeStruct((BATCH, n_tiles * TILE, D_MODEL), F32),
                   jax.ShapeDtypeStruct((BATCH, n_tiles * TILE, D_MODEL), BF16)],
        compiler_params=pltpu.CompilerParams(
            dimension_semantics=("parallel", "parallel"), vmem_limit_bytes=VMEM_LIMIT),
        name="out_proj",
    )(x_all, o_f, o_b, gr, b_att, c_att, mod, w, on, post, pre)


def _ffn_kernel(hp_ref, h_ref, hn_ref, x1_ref, mod_ref, wu_ref, cw_ref, cb_ref, wd_ref, post_ref,
                o_ref, *, seg_first, seg_last):
    i = pl.program_id(1)
    has_prev = functools.reduce(lambda a, t: a & (i != t), seg_first, True)
    has_next = functools.reduce(lambda a, t: a & (i != t), seg_last, True)
    hp = jnp.where(has_prev, hp_ref[0], 0)
    hn = jnp.where(has_next, hn_ref[0], 0)
    h_ext = jnp.concatenate([hp, h_ref[0], hn], axis=0)
    ext = TILE + 2 * HALO
    mid = slice(HALO, HALO + TILE)

    def conv(u, c0):
        cw = cw_ref[:, c0:c0 + FFN_CHUNK]
        prev = pltpu.roll(u, 1, 0)[mid]
        nxt = pltpu.roll(u, ext - 1, 0)[mid]
        return cw[0:1] * prev + cw[1:2] * u[mid] + cw[2:3] * nxt + cb_ref[:, c0:c0 + FFN_CHUNK]

    acc = jnp.zeros((TILE, D_MODEL), F32)
    for fc in range(FFN_DIM // FFN_CHUNK):
        ca = fc * FFN_CHUNK
        cg = FFN_DIM + ca
        a = conv(_dot(h_ext, wu_ref[:, ca:ca + FFN_CHUNK]), ca)
        g = conv(_dot(h_ext, wu_ref[:, cg:cg + FFN_CHUNK]), cg)
        act = (_silu(a) * g).astype(BF16)
        acc = acc + _dot(act, wd_ref[ca:ca + FFN_CHUNK, :])
    g2 = mod_ref[0, 0, 5:6, :]
    o_ref[0] = x1_ref[0] + g2 * _row_rms(acc, post_ref[...])


def _ffn(h2, x1, mod, layer, wu, cw, cb, wd, post, with_ctx):
    t0 = 0 if with_ctx else 1
    n_tiles = N_TILES - t0
    per = TILE // HALO
    n_halo = n_tiles * per
    seg_first = (0, 1) if with_ctx else (0,)
    seg_last = (0, n_tiles - 1) if with_ctx else (n_tiles - 1,)

    def tok(width):
        return pl.BlockSpec((1, TILE, width), lambda b, i: (b, i, 0))

    def const(shape):
        return pl.BlockSpec(shape, lambda b, i: (0,) * len(shape))

    prev_spec = pl.BlockSpec((1, HALO, D_MODEL), lambda b, i: (b, jnp.maximum(i * per - 1, 0), 0))
    next_spec = pl.BlockSpec((1, HALO, D_MODEL),
                             lambda b, i: (b, jnp.minimum((i + 1) * per, n_halo - 1), 0))
    return pl.pallas_call(
        functools.partial(_ffn_kernel, seg_first=seg_first, seg_last=seg_last),
        grid=(BATCH, n_tiles),
        in_specs=[prev_spec, tok(D_MODEL), next_spec, tok(D_MODEL), _mod_spec(layer, t0),
                  const((D_MODEL, 2 * FFN_DIM)), const((3, 2 * FFN_DIM)), const((1, 2 * FFN_DIM)),
                  const((FFN_DIM, D_MODEL)), const((1, D_MODEL))],
        out_specs=tok(D_MODEL),
        out_shape=jax.ShapeDtypeStruct((BATCH, n_tiles * TILE, D_MODEL), F32),
        compiler_params=pltpu.CompilerParams(
            dimension_semantics=("parallel", "parallel"), vmem_limit_bytes=VMEM_LIMIT),
        name="ffn",
    )(h2, h2, h2, x1, mod, wu, cw, cb, wd, post)


def _rope_tables():
    rows = SEQ // GRID_W
    row = jnp.repeat(jnp.arange(rows), GRID_W).astype(F32)
    col = (jnp.arange(rows * GRID_W) % GRID_W).astype(F32)
    n_freq = HEAD_DIM // 4
    inv_freq = ROPE_THETA ** (-jnp.arange(n_freq, dtype=F32) / n_freq)
    ang_r = row[:, None] * inv_freq[None, :]
    ang_c = col[:, None] * inv_freq[None, :]
    ang = jnp.concatenate([ang_r, ang_r, ang_c, ang_c], axis=-1)
    cos = jnp.tile(jnp.cos(ang), (1, LANES // HEAD_DIM))
    sin = jnp.tile(jnp.sin(ang), (1, LANES // HEAD_DIM))
    upper = (jnp.arange(LANES) % 32) < 16
    sin_up = jnp.where(upper[None, :], -sin, 0.0)
    sin_dn = jnp.where(upper[None, :], 0.0, sin)
    pad = jnp.zeros((CTX_LEN, LANES), F32)
    return (jnp.concatenate([jnp.ones((CTX_LEN, LANES), F32), cos], axis=0),
            jnp.concatenate([pad, sin_up], axis=0), jnp.concatenate([pad, sin_dn], axis=0))


def _permute_heads(w, axis):
    idx = jnp.concatenate([jnp.arange(h * HEAD_DIM, (h + 1) * HEAD_DIM) for h in HEAD_PERM])
    return jnp.take(w, idx, axis=axis)


def _layout_w_in(w):
    sizes = (GLA_QK, GLA_QK, GLA_WIDTH, GLA_WIDTH, 2 * GLA_GATE_RANK,
             Q_WIDTH, KV_WIDTH, KV_WIDTH, Q_WIDTH, KV_WIDTH, KV_WIDTH)
    parts, start = [], 0
    for size in sizes:
        parts.append(w[:, start:start + size])
        start += size
    a_q, a_k, a_v, a_r, a_z, b_q, b_k, b_v, c_q, c_k, c_v = parts
    z_pad = jnp.zeros((w.shape[0], LANES - 2 * GLA_GATE_RANK), w.dtype)
    return jnp.concatenate(
        [a_q, a_k, a_v, a_r, _permute_heads(b_q, 1), b_k, b_v, _permute_heads(c_q, 1), c_k, c_v,
         a_z, z_pad], axis=1).astype(BF16)


def _layout_w_out(w):
    a = w[:GLA_WIDTH]
    b = _permute_heads(w[GLA_WIDTH:GLA_WIDTH + Q_WIDTH], 0)
    c = _permute_heads(w[GLA_WIDTH + Q_WIDTH:], 0)
    return jnp.concatenate([a, b, c], axis=0).astype(BF16)


def _layout_gate(w_gate):
    out = jnp.zeros((2, LANES, GLA_QK), w_gate.dtype)
    for d in range(2):
        out = out.at[d, d * GLA_GATE_RANK:(d + 1) * GLA_GATE_RANK].set(w_gate[d])
    return out.astype(BF16)


def _diag_reduce_matrix():
    r = jnp.arange(GLA_SUB * LANES)
    c = jnp.arange(GLA_HEADS * GLA_SUB)
    same_s = (r[:, None] // LANES) == (c[None, :] % GLA_SUB)
    same_h = ((r[:, None] % LANES) // GLA_DK) == (c[None, :] // GLA_SUB)
    return jnp.where(same_s & same_h, 1.0, 0.0).astype(BF16)


def kernel(x, c, ctx, c_ctx, w_mod, b_mod, attn_pre_norm, attn_post_norm, ffn_pre_norm,
           ffn_post_norm, w_in, gla_w_gate, gla_b_gate, gla_out_norm, swa_sink, gqa_q_norm,
           gqa_k_norm, w_out, ffn_w_up, ffn_conv_w, ffn_conv_b, ffn_w_down):
    assert x.shape == (BATCH, SEQ, D_MODEL) and ctx.shape == (BATCH, CTX_LEN, D_MODEL)
    cos, sin_up, sin_dn = _rope_tables()
    wst = _diag_reduce_matrix()
    cs = jnp.concatenate([c, c_ctx[None, :], jnp.zeros((16 - BATCH - 1, D_MODEL), F32)], axis=0)
    mod = _modulation(cs, w_mod, b_mod).reshape(DEPTH, 16, MOD_CHUNKS, D_MODEL)

    x_all = jnp.concatenate([ctx, x], axis=1)
    for layer in range(DEPTH):
        with_ctx = layer < DEPTH - 1
        row = lambda v: v[layer][None, :]
        tile2 = lambda v: jnp.tile(v[layer], LANES // HEAD_DIM)[None, :]
        gq, gk, gv, gr, gz, bq, bk, bv, cq, ck, cv = _in_proj(
            x_all, mod, layer, row(attn_pre_norm), _layout_w_in(w_in[layer]), cos, sin_up, sin_dn,
            tile2(gqa_q_norm), tile2(gqa_k_norm))
        o_f, o_b = _gla(gq, gk, gv, gz, _layout_gate(gla_w_gate[layer]),
                        gla_b_gate[layer][:, None, :], wst)
        b_att = _swa(swa_sink[layer].reshape(KV_HEADS, GROUP), bq, bk, bv, with_ctx)
        c_att = _gqa(cq, ck, cv, with_ctx)
        x1, h2 = _out_proj(x_all, o_f, o_b, gr, b_att, c_att, mod, layer,
                           _layout_w_out(w_out[layer]), tile2(gla_out_norm),
                           row(attn_post_norm), row(ffn_pre_norm), with_ctx)
        x_all = _ffn(h2, x1, mod, layer, ffn_w_up[layer].astype(BF16), ffn_conv_w[layer],
                     row(ffn_conv_b), ffn_w_down[layer].astype(BF16), row(ffn_post_norm), with_ctx)
    return x_all
```

```python
import functools
import itertools

import jax
import jax.numpy as jnp
from jax import lax
from jax.experimental import pallas as pl
from jax.experimental.pallas import tpu as pltpu

F32 = jnp.float32
BF16 = jnp.bfloat16

D_MODEL = 1024
BATCH = 8
SEQ = 2048
DEPTH = 2
CTX_LEN = 256
TOK = CTX_LEN + SEQ
GRID_W = 64
HEAD_DIM = 64
ROPE_THETA = 10000.0
NORM_EPS = 1e-6
MOD_CHUNKS = 6

GLA_HEADS = 4
GLA_DK = 32
GLA_DV = 64
GLA_GATE_RANK = 16
GLA_GATE_TAU = 16.0
GLA_QK = GLA_HEADS * GLA_DK
GLA_WIDTH = GLA_HEADS * GLA_DV

Q_HEADS = 6
KV_HEADS = 2
GROUP = Q_HEADS // KV_HEADS
Q_WIDTH = Q_HEADS * HEAD_DIM
KV_WIDTH = KV_HEADS * HEAD_DIM
SWA_BLOCK = 128

FFN_DIM = 2816
FFN_CHUNK = 256

LANES = 128
TILE = 256
N_TILES = TOK // TILE
HALO = 16
GLA_BLOCK = 256
GLA_CHUNK = 64
GLA_SUB = 16
ATT_TQ = 256
VMEM_LIMIT = 56 * 1024 * 1024

C_GQ, C_GK, C_GV, C_GR = 0, 128, 256, 512
C_BQ, C_BK, C_BV = 768, 1152, 1280
C_CQ, C_CK, C_CV = 1408, 1792, 1920
C_GZ = 2048
IN_COLS = 2176
HEAD_PERM = (0, 3, 1, 4, 2, 5)

_NT = (((1,), (1,)), ((), ()))


def _dot(a, b):
    return jnp.dot(a, b, preferred_element_type=F32)


def _dot_nt(a, b):
    return lax.dot_general(a, b, _NT, preferred_element_type=F32)


def _idiv(x, n):
    assert n & (n - 1) == 0
    return lax.shift_right_logical(x, jnp.int32(n.bit_length() - 1))


def _split2(x):
    hi = x.astype(BF16)
    lo = (x - hi.astype(F32)).astype(BF16)
    return hi, lo


def _split3(x):
    h1 = x.astype(BF16)
    r1 = x - h1.astype(F32)
    h2 = r1.astype(BF16)
    h3 = (r1 - h2.astype(F32)).astype(BF16)
    return h1, h2, h3


def _seg_ones(width, seg):
    r = _idiv(lax.broadcasted_iota(jnp.int32, (width, width), 0), seg)
    c = _idiv(lax.broadcasted_iota(jnp.int32, (width, width), 1), seg)
    return jnp.where(r == c, 1.0, 0.0).astype(BF16)


def _seg_sum(x, ones):
    hi, lo = _split2(x)
    return _dot(hi, ones) + _dot(lo, ones)


def _row_rms(x, w):
    ms = jnp.mean(x * x, axis=-1, keepdims=True)
    return x * lax.rsqrt(ms + NORM_EPS) * w


def _head_rms(x, ones, w):
    ss = _seg_sum(x * x, ones)
    return x * lax.rsqrt(ss * (1.0 / HEAD_DIM) + NORM_EPS) * w


def _silu(x):
    return x / (1.0 + jnp.exp(-x))


def _rope(x, cos, sin_up, sin_dn):
    return x * cos + pltpu.roll(x, LANES - 16, 1) * sin_up + pltpu.roll(x, 16, 1) * sin_dn


def _mod_kernel(c_ref, w_ref, b_ref, o_ref):
    a = _silu(c_ref[...]).astype(BF16)
    o_ref[0] = _dot(a, w_ref[0].astype(BF16)) + b_ref[0]


def _modulation(cs, w_mod, b_mod):
    tn = 1536
    n = MOD_CHUNKS * D_MODEL
    return pl.pallas_call(
        _mod_kernel,
        grid=(DEPTH, n // tn),
        in_specs=[
            pl.BlockSpec((16, D_MODEL), lambda l, j: (0, 0)),
            pl.BlockSpec((1, D_MODEL, tn), lambda l, j: (l, 0, j)),
            pl.BlockSpec((1, 1, tn), lambda l, j: (l, 0, j)),
        ],
        out_specs=pl.BlockSpec((1, 16, tn), lambda l, j: (l, 0, j)),
        out_shape=jax.ShapeDtypeStruct((DEPTH, 16, n), F32),
        compiler_params=pltpu.CompilerParams(
            dimension_semantics=("arbitrary", "arbitrary"), vmem_limit_bytes=VMEM_LIMIT),
        name="modulation",
    )(cs, w_mod, b_mod.reshape(DEPTH, 1, n))


def _mod_spec(layer, t0):
    return pl.BlockSpec(
        (1, 1, MOD_CHUNKS, D_MODEL),
        lambda b, i: (layer, jnp.where(i + t0 == 0, BATCH, b), 0, 0))


def _in_proj_kernel(x_ref, mod_ref, nw_ref, w_ref, cos_ref, su_ref, sd_ref, qn_ref, kn_ref,
                    gq_ref, gk_ref, gv_ref, gr_ref, gz_ref,
                    bq_ref, bk_ref, bv_ref, cq_ref, ck_ref, cv_ref):
    x = x_ref[0]
    sh = mod_ref[0, 0, 0:1, :]
    sc = mod_ref[0, 0, 1:2, :]
    h = (_row_rms(x, nw_ref[...]) * (1.0 + sc) + sh).astype(BF16)

    def proj(c0, width):
        return _dot(h, w_ref[:, c0:c0 + width])

    cos, su, sd = cos_ref[...], su_ref[...], sd_ref[...]
    ones = _seg_ones(LANES, HEAD_DIM)
    att_scale = HEAD_DIM ** -0.5

    att_w = Q_WIDTH + 2 * KV_WIDTH
    p_a = proj(C_GQ, C_BQ - C_GQ)
    p_b = proj(C_BQ, att_w)

    gq_ref[0] = p_a[:, C_GQ:C_GQ + GLA_QK] * (GLA_DK ** -0.5)
    gk_ref[0] = p_a[:, C_GK:C_GK + GLA_QK]
    gv_ref[0] = p_a[:, C_GV:C_GV + GLA_WIDTH]
    gr_ref[0] = p_a[:, C_GR:C_GR + GLA_WIDTH]

    p_c = proj(C_CQ, att_w)

    for g in range(GROUP):
        ls = slice(g * LANES, (g + 1) * LANES)
        bq_ref[0, :, ls] = (_rope(p_b[:, ls], cos, su, sd) * att_scale).astype(BF16)
    bk_ref[0] = _rope(p_b[:, Q_WIDTH:Q_WIDTH + KV_WIDTH], cos, su, sd).astype(BF16)
    bv_ref[0] = p_b[:, Q_WIDTH + KV_WIDTH:].astype(BF16)

    gz_ref[0] = proj(C_GZ, LANES)

    for g in range(GROUP):
        ls = slice(g * LANES, (g + 1) * LANES)
        cq = _head_rms(p_c[:, ls], ones, qn_ref[...])
        cq_ref[0, :, ls] = (_rope(cq, cos, su, sd) * att_scale).astype(BF16)
    ck = _head_rms(p_c[:, Q_WIDTH:Q_WIDTH + KV_WIDTH], ones, kn_ref[...])
    ck_ref[0] = _rope(ck, cos, su, sd).astype(BF16)
    cv_ref[0] = p_c[:, Q_WIDTH + KV_WIDTH:].astype(BF16)


def _in_proj(x_all, mod, layer, nw, w, cos, su, sd, qn, kn):
    def tok(width):
        return pl.BlockSpec((1, TILE, width), lambda b, i: (b, i, 0))

    def const(shape):
        return pl.BlockSpec(shape, lambda b, i: (0,) * len(shape))

    def tab():
        return pl.BlockSpec((TILE, LANES), lambda b, i: (i, 0))

    widths = [(GLA_QK, F32), (GLA_QK, F32), (GLA_WIDTH, F32), (GLA_WIDTH, F32), (LANES, F32),
              (Q_WIDTH, BF16), (KV_WIDTH, BF16), (KV_WIDTH, BF16),
              (Q_WIDTH, BF16), (KV_WIDTH, BF16), (KV_WIDTH, BF16)]
    return pl.pallas_call(
        _in_proj_kernel,
        grid=(BATCH, N_TILES),
        in_specs=[tok(D_MODEL), _mod_spec(layer, 0), const((1, D_MODEL)), const((D_MODEL, IN_COLS)),
                  tab(), tab(), tab(), const((1, LANES)), const((1, LANES))],
        out_specs=[tok(wd) for wd, _ in widths],
        out_shape=[jax.ShapeDtypeStruct((BATCH, TOK, wd), dt) for wd, dt in widths],
        compiler_params=pltpu.CompilerParams(
            dimension_semantics=("parallel", "parallel"), vmem_limit_bytes=VMEM_LIMIT),
        name="in_proj",
    )(x_all, mod, nw, w, cos, su, sd, qn, kn)


def _lane_group_mask(width, group, h):
    lane = lax.broadcasted_iota(jnp.int32, (1, width), 1)
    return _idiv(lane, group) == h


def _gla_direction(q_ref, k_ref, v_ref, z_ref, wg, bg, wst_ref, o_ref, st_ref, b_s, e_s, rev):
    nb, c_len, sb = GLA_BLOCK, GLA_CHUNK, GLA_SUB
    n_chunks, n_sub = nb // c_len, c_len // sb

    g = _dot(z_ref[0].astype(BF16), wg) + bg
    la = (jnp.minimum(g, 0.0) - jnp.log1p(jnp.exp(-jnp.abs(g)))) * (1.0 / GLA_GATE_TAU)

    row = lax.broadcasted_iota(jnp.int32, (nb, nb), 0)
    col = lax.broadcasted_iota(jnp.int32, (nb, nb), 1)
    inside = _idiv(row, c_len) == _idiv(col, c_len)
    tri = jnp.where(inside & ((col >= row) if rev else (col <= row)), 1.0, 0.0).astype(BF16)
    h1, h2, h3 = _split3(la)
    b = _dot(tri, h1) + _dot(tri, h2) + _dot(tri, h3)
    b_s[...] = b
    yield

    q, k, v = q_ref[0], k_ref[0], v_ref[0]

    t_loc = lax.broadcasted_iota(jnp.int32, (sb, 1), 0)

    def diag_block(jb, carry):
        r0 = pl.multiple_of(jb * sb, sb)
        bb = b_s[pl.ds(r0, sb), :]
        qb = q_ref[0, pl.ds(r0, sb), :]
        for s in range(sb):
            bs = b_s[pl.ds(r0 + s, 1), :]
            ks = k_ref[0, pl.ds(r0 + s, 1), :]
            e = qb * ks * jnp.exp(jnp.minimum(bb - bs, 0.0))
            keep = (t_loc <= s) if rev else (t_loc >= s)
            e_s[pl.ds(r0, sb), s * LANES:(s + 1) * LANES] = jnp.where(keep, e, 0.0).astype(BF16)
        return carry

    lax.fori_loop(0, nb // sb, diag_block, 0)
    a_diag = _dot(e_s[...], wst_ref[...])
    yield

    hm_k = [_lane_group_mask(GLA_QK, GLA_DK, h) for h in range(GLA_HEADS)]
    hm_v = [_lane_group_mask(GLA_WIDTH, GLA_DV, h) for h in range(GLA_HEADS)]
    st_row = _idiv(lax.broadcasted_iota(jnp.int32, (GLA_WIDTH, GLA_QK), 0), GLA_DV)
    st_col = _idiv(lax.broadcasted_iota(jnp.int32, (GLA_WIDTH, GLA_QK), 1), GLA_DK)
    st_mask = st_row == st_col
    v_t = jnp.transpose(v).astype(BF16)
    tok_row = _idiv(lax.broadcasted_iota(jnp.int32, (nb, 1), 0), c_len)

    q_hat, kv_t, d_row, o_intra = [], [], [], []
    for c in range(n_chunks):
        cs = slice(c * c_len, (c + 1) * c_len)
        bc, qc, kc, vc = b[cs], q[cs], k[cs], v[cs]
        end = 0 if rev else c_len - 1
        b_end = bc[end:end + 1]
        q_hat.append((qc * jnp.exp(bc)).astype(BF16))
        d_row.append(jnp.exp(b_end))
        k_hat = jnp.where(tok_row == c, k * jnp.exp(jnp.minimum(b_end - b, 0.0)), 0.0)
        kv_t.append(jnp.where(st_mask, _dot(v_t, k_hat.astype(BF16)), 0.0))
        yield

        blocks = [jnp.zeros((sb, GLA_WIDTH), F32) for _ in range(n_sub)]
        for j in range(n_sub):
            rb = slice(j * sb, (j + 1) * sb)
            v_exp = jnp.concatenate(
                [jnp.where(hm_v[h], vc[rb], 0.0) for h in range(GLA_HEADS)], axis=0).astype(BF16)
            ad = a_diag[c * c_len + j * sb:c * c_len + (j + 1) * sb]
            has_later = (j > 0) if rev else (j < n_sub - 1)
            if has_later:
                edge = j * sb if rev else (j + 1) * sb - 1
                ref_b = bc[edge:edge + 1]
                later = slice(0, j * sb) if rev else slice((j + 1) * sb, c_len)
                q_l = (qc[later] * jnp.exp(bc[later] - ref_b)).astype(BF16)
                k_j = kc[rb] * jnp.exp(ref_b - bc[rb])
                k_tile = jnp.concatenate(
                    [jnp.where(hm_k[h], k_j, 0.0) for h in range(GLA_HEADS)], axis=0).astype(BF16)
                sc = _dot_nt(q_l, k_tile)
                p = jnp.concatenate([sc, ad], axis=0) if rev else jnp.concatenate([ad, sc], axis=0)
                first = 0 if rev else j
            else:
                p, first = ad, j
            contrib = _dot(p.astype(BF16), v_exp)
            for r in range(contrib.shape[0] // sb):
                blocks[first + r] = blocks[first + r] + contrib[r * sb:(r + 1) * sb]
            yield
        o_intra.append(jnp.concatenate(blocks, axis=0))

    state = st_ref[...]
    outs = [None] * n_chunks
    for c in (reversed(range(n_chunks)) if rev else range(n_chunks)):
        outs[c] = o_intra[c] + _dot_nt(q_hat[c], state.astype(BF16))
        state = state * d_row[c] + kv_t[c]
        yield
    st_ref[...] = state
    o_ref[0, 0] = jnp.concatenate(outs, axis=0)


def _gla_kernel(qf, kf, vf, zf, qb, kb, vb, zb, wg_ref, bg_ref, wst_ref, of_ref, ob_ref,
                sf_ref, sb_ref, bf_s, bb_s, ef_s, eb_s):
    @pl.when(pl.program_id(1) == 0)
    def _():
        sf_ref[...] = jnp.zeros_like(sf_ref)
        sb_ref[...] = jnp.zeros_like(sb_ref)

    streams = [
        _gla_direction(qf, kf, vf, zf, wg_ref[0], bg_ref[0], wst_ref, of_ref, sf_ref, bf_s, ef_s, False),
        _gla_direction(qb, kb, vb, zb, wg_ref[1], bg_ref[1], wst_ref, ob_ref, sb_ref, bb_s, eb_s, True)]
    for _ in itertools.zip_longest(*streams):
        pass


def _gla(gq, gk, gv, gz, wg, bg, wst):
    nblk = TOK // GLA_BLOCK

    def fwd(b, s):
        return (b, s, 0)

    def bwd(b, s):
        return (b, jnp.where(s == 0, 0, nblk - s), 0)

    def specs(index_map):
        return [pl.BlockSpec((1, GLA_BLOCK, wd), index_map)
                for wd in (GLA_QK, GLA_QK, GLA_WIDTH, LANES)]

    def const(shape):
        return pl.BlockSpec(shape, lambda b, s: (0,) * len(shape))

    return pl.pallas_call(
        _gla_kernel,
        grid=(BATCH, nblk),
        in_specs=specs(fwd) + specs(bwd) + [const((2, LANES, GLA_QK)), const((2, 1, GLA_QK)),
                                            const((GLA_SUB * LANES, GLA_HEADS * GLA_SUB))],
        out_specs=[pl.BlockSpec((1, 1, GLA_BLOCK, GLA_WIDTH), lambda b, s: (0,) + fwd(b, s)),
                   pl.BlockSpec((1, 1, GLA_BLOCK, GLA_WIDTH), lambda b, s: (0,) + bwd(b, s))],
        out_shape=[jax.ShapeDtypeStruct((1, BATCH, TOK, GLA_WIDTH), F32)] * 2,
        scratch_shapes=[pltpu.VMEM((GLA_WIDTH, GLA_QK), F32), pltpu.VMEM((GLA_WIDTH, GLA_QK), F32),
                        pltpu.VMEM((GLA_BLOCK, GLA_QK), F32), pltpu.VMEM((GLA_BLOCK, GLA_QK), F32),
                        pltpu.VMEM((GLA_BLOCK, GLA_SUB * LANES), BF16),
                        pltpu.VMEM((GLA_BLOCK, GLA_SUB * LANES), BF16)],
        compiler_params=pltpu.CompilerParams(
            dimension_semantics=("parallel", "arbitrary"), vmem_limit_bytes=VMEM_LIMIT),
        name="gla",
    )(gq, gk, gv, gz, gq, gk, gv, gz, wg, bg, wst)


def _attend(q_ref, keys, values, sink_ref, bias, groups):
    tq = q_ref.shape[1]
    lane = lax.broadcasted_iota(jnp.int32, (1, LANES), 1)
    first = lane < HEAD_DIM
    units = [(h, gs) for gs in groups for h in range(KV_HEADS)]

    def scores(h, gs):
        mine = first if h == 0 else jnp.logical_not(first)
        qz = [jnp.where(mine, q_ref[0, :, g * LANES:(g + 1) * LANES], 0) for g in gs]
        return _dot_nt(qz[0] if len(gs) == 1 else jnp.concatenate(qz, axis=0), keys)

    outs = {}
    s_next = scores(*units[0])
    for idx, (h, gs) in enumerate(units):
        s = s_next
        if idx + 1 < len(units):
            s_next = scores(*units[idx + 1])
        if bias is not None:
            nb = bias.shape[1]
            s = jnp.concatenate([s[:, :nb] + bias[:s.shape[0]], s[:, nb:]], axis=1)
        m = jnp.max(s, axis=-1, keepdims=True)
        if sink_ref is not None:
            sk = [jnp.full((tq, 1), sink_ref[h, g], F32) for g in gs]
            sk = sk[0] if len(gs) == 1 else jnp.concatenate(sk, axis=0)
            m = jnp.maximum(m, sk)
        p = jnp.exp(s - m)
        den = jnp.sum(p, axis=-1, keepdims=True)
        if sink_ref is not None:
            den = den + jnp.exp(sk - m)
        o = _dot(p.astype(BF16), values) / den
        for r, g in enumerate(gs):
            outs[(h, g)] = o[r * tq:(r + 1) * tq]
    return jnp.concatenate(
        [jnp.where(first, outs[(0, g)], outs[(1, g)]) for g in range(GROUP)], axis=1)


_ALL_GROUPS = (tuple(range(GROUP)),)
_EACH_GROUP = tuple((g,) for g in range(GROUP))


def _swa_kernel(sink_ref, band_ref, q_ref, kc_ref, vc_ref, kp_ref, kk_ref, kn_ref, vp_ref, vk_ref,
                vn_ref, o_ref, *, ctx_blocks, n_lat):
    i = pl.program_id(1)
    w = SWA_BLOCK

    @pl.when(i < ctx_blocks)
    def _():
        o_ref[0] = _attend(q_ref, kc_ref[0], vc_ref[0], sink_ref, None,
                           _ALL_GROUPS).astype(o_ref.dtype)

    @pl.when(i >= ctx_blocks)
    def _():
        li = i - ctx_blocks
        keys = jnp.concatenate([kp_ref[0], kk_ref[0], kn_ref[0], kc_ref[0]], axis=0)
        values = jnp.concatenate([vp_ref[0], vk_ref[0], vn_ref[0], vc_ref[0]], axis=0)
        col = lax.broadcasted_iota(jnp.int32, (1, 3 * w), 1)
        gone = ((col < w) & (li == 0)) | ((col >= 2 * w) & (li == n_lat - 1))
        bias = band_ref[...] + jnp.where(gone, -1e30, 0.0)
        o_ref[0] = _attend(q_ref, keys, values, sink_ref, bias, _ALL_GROUPS).astype(o_ref.dtype)


def _swa(sink, bq, bk, bv, with_ctx):
    w = SWA_BLOCK
    ctx_blocks = CTX_LEN // w
    n_lat = SEQ // w
    q_off = 0 if with_ctx else ctx_blocks
    n_steps = n_lat + (ctx_blocks if with_ctx else 0)
    cb = ctx_blocks if with_ctx else 0

    def lat_block(shift):
        def index_map(b, i):
            li = jnp.clip(i - cb + shift, 0, n_lat - 1)
            return (b, li + ctx_blocks, 0)
        return pl.BlockSpec((1, w, KV_WIDTH), index_map)

    t = jnp.arange(GROUP * w)[:, None] % w
    kk = jnp.arange(3 * w)[None, :]
    band = jnp.where((kk - t >= 0) & (kk - t <= 2 * w), 0.0, -1e30).astype(F32)

    ctx_spec = pl.BlockSpec((1, CTX_LEN, KV_WIDTH), lambda b, i: (b, 0, 0))
    return pl.pallas_call(
        functools.partial(_swa_kernel, ctx_blocks=cb, n_lat=n_lat),
        grid=(BATCH, n_steps),
        in_specs=[pl.BlockSpec(memory_space=pltpu.SMEM),
                  pl.BlockSpec((GROUP * w, 3 * w), lambda b, i: (0, 0)),
                  pl.BlockSpec((1, w, Q_WIDTH), lambda b, i: (b, i + q_off, 0)),
                  ctx_spec, ctx_spec,
                  lat_block(-1), lat_block(0), lat_block(1),
                  lat_block(-1), lat_block(0), lat_block(1)],
        out_specs=pl.BlockSpec((1, w, Q_WIDTH), lambda b, i: (b, i, 0)),
        out_shape=jax.ShapeDtypeStruct((BATCH, n_steps * w, Q_WIDTH), BF16),
        compiler_params=pltpu.CompilerParams(
            dimension_semantics=("parallel", "parallel"), vmem_limit_bytes=VMEM_LIMIT),
        name="swa",
    )(sink, band, bq, bk, bv, bk, bk, bk, bv, bv, bv)


def _gqa_kernel(q_ref, k_ref, v_ref, o_ref, *, ctx_blocks):
    i = pl.program_id(1)

    @pl.when(i < ctx_blocks)
    def _():
        o_ref[0] = _attend(q_ref, k_ref[0, 0:CTX_LEN, :], v_ref[0, 0:CTX_LEN, :],
                           None, None, _EACH_GROUP).astype(o_ref.dtype)

    @pl.when(i >= ctx_blocks)
    def _():
        o_ref[0] = _attend(q_ref, k_ref[0], v_ref[0], None, None, _EACH_GROUP).astype(o_ref.dtype)


def _gqa(cq, ck, cv, with_ctx):
    tq = ATT_TQ
    ctx_blocks = CTX_LEN // tq
    q_off = 0 if with_ctx else ctx_blocks
    n_steps = SEQ // tq + (ctx_blocks if with_ctx else 0)
    kv_spec = pl.BlockSpec((1, TOK, KV_WIDTH), lambda b, i: (b, 0, 0))
    return pl.pallas_call(
        functools.partial(_gqa_kernel, ctx_blocks=ctx_blocks if with_ctx else 0),
        grid=(BATCH, n_steps),
        in_specs=[pl.BlockSpec((1, tq, Q_WIDTH), lambda b, i: (b, i + q_off, 0)), kv_spec, kv_spec],
        out_specs=pl.BlockSpec((1, tq, Q_WIDTH), lambda b, i: (b, i, 0)),
        out_shape=jax.ShapeDtypeStruct((BATCH, n_steps * tq, Q_WIDTH), BF16),
        compiler_params=pltpu.CompilerParams(
            dimension_semantics=("parallel", "parallel"), vmem_limit_bytes=VMEM_LIMIT),
        name="gqa",
    )(cq, ck, cv)


def _out_proj_kernel(x_ref, of_ref, ob_ref, r_ref, b_ref, c_ref, mod_ref, w_ref, on_ref,
                     post_ref, pre_ref, x1_ref, h2_ref):
    ones = _seg_ones(LANES, GLA_DV)
    mix = None
    for part in range(GLA_WIDTH // LANES):
        ls = slice(part * LANES, (part + 1) * LANES)
        o = of_ref[0, 0, :, ls] + ob_ref[0, 0, :, ls]
        ya = _head_rms(o, ones, on_ref[...]) * _silu(r_ref[0, :, ls])
        term = _dot(ya.astype(BF16), w_ref[ls, :])
        mix = term if mix is None else mix + term
    mix = mix + _dot(b_ref[0], w_ref[GLA_WIDTH:GLA_WIDTH + Q_WIDTH, :])
    mix = mix + _dot(c_ref[0], w_ref[GLA_WIDTH + Q_WIDTH:, :])
    g1 = mod_ref[0, 0, 2:3, :]
    sh2 = mod_ref[0, 0, 3:4, :]
    sc2 = mod_ref[0, 0, 4:5, :]
    x1 = x_ref[0] + g1 * _row_rms(mix, post_ref[...])
    x1_ref[0] = x1
    h2_ref[0] = (_row_rms(x1, pre_ref[...]) * (1.0 + sc2) + sh2).astype(BF16)


def _out_proj(x_all, o_f, o_b, gr, b_att, c_att, mod, layer, w, on, post, pre, with_ctx):
    t0 = 0 if with_ctx else 1
    n_tiles = N_TILES - t0

    def tok(width, off):
        return pl.BlockSpec((1, TILE, width), lambda b, i: (b, i + off, 0))

    def const(shape):
        return pl.BlockSpec(shape, lambda b, i: (0,) * len(shape))

    gla_spec = pl.BlockSpec((1, 1, TILE, GLA_WIDTH), lambda b, i: (0, b, i + t0, 0))
    return pl.pallas_call(
        _out_proj_kernel,
        grid=(BATCH, n_tiles),
        in_specs=[tok(D_MODEL, t0), gla_spec, gla_spec, tok(GLA_WIDTH, t0),
                  tok(Q_WIDTH, 0), tok(Q_WIDTH, 0), _mod_spec(layer, t0),
                  const((D_MODEL, D_MODEL)), const((1, LANES)), const((1, D_MODEL)),
                  const((1, D_MODEL))],
        out_specs=[tok(D_MODEL, 0), tok(D_MODEL, 0)],
        out_shape=[jax.ShapeDtypeStruct((BATCH, n_tiles * TILE, D_MODEL), F32),
                   jax.ShapeDtypeStruct((BATCH, n_tiles * TILE, D_MODEL), BF16)],
        compiler_params=pltpu.CompilerParams(
            dimension_semantics=("parallel", "parallel"), vmem_limit_bytes=VMEM_LIMIT),
        name="out_proj",
    )(x_all, o_f, o_b, gr, b_att, c_att, mod, w, on, post, pre)


def _ffn_kernel(hp_ref, h_ref, hn_ref, x1_ref, mod_ref, wu_ref, cw_ref, cb_ref, wd_ref, post_ref,
                o_ref, *, seg_first, seg_last):
    i = pl.program_id(1)
    has_prev = functools.reduce(lambda a, t: a & (i != t), seg_first, True)
    has_next = functools.reduce(lambda a, t: a & (i != t), seg_last, True)
    hp = jnp.where(has_prev, hp_ref[0], 0)
    hn = jnp.where(has_next, hn_ref[0], 0)
    h_ext = jnp.concatenate([hp, h_ref[0], hn], axis=0)
    ext = TILE + 2 * HALO
    mid = slice(HALO, HALO + TILE)

    def conv(u, c0):
        cw = cw_ref[:, c0:c0 + FFN_CHUNK]
        prev = pltpu.roll(u, 1, 0)[mid]
        nxt = pltpu.roll(u, ext - 1, 0)[mid]
        return cw[0:1] * prev + cw[1:2] * u[mid] + cw[2:3] * nxt + cb_ref[:, c0:c0 + FFN_CHUNK]

    n_fc = FFN_DIM // FFN_CHUNK

    def up(fc):
        ca = fc * FFN_CHUNK
        cg = FFN_DIM + ca
        return (_dot(h_ext, wu_ref[:, ca:ca + FFN_CHUNK]), _dot(h_ext, wu_ref[:, cg:cg + FFN_CHUNK]))

    acc = jnp.zeros((TILE, D_MODEL), F32)
    ua, ug = up(0)
    for fc in range(n_fc):
        nxt_u = up(fc + 1) if fc + 1 < n_fc else None
        ca = fc * FFN_CHUNK
        act = (_silu(conv(ua, ca)) * conv(ug, FFN_DIM + ca)).astype(BF16)
        acc = acc + _dot(act, wd_ref[ca:ca + FFN_CHUNK, :])
        if nxt_u is not None:
            ua, ug = nxt_u
    g2 = mod_ref[0, 0, 5:6, :]
    o_ref[0] = x1_ref[0] + g2 * _row_rms(acc, post_ref[...])


def _ffn(h2, x1, mod, layer, wu, cw, cb, wd, post, with_ctx):
    t0 = 0 if with_ctx else 1
    n_tiles = N_TILES - t0
    per = TILE // HALO
    n_halo = n_tiles * per
    seg_first = (0, 1) if with_ctx else (0,)
    seg_last = (0, n_tiles - 1) if with_ctx else (n_tiles - 1,)

    def tok(width):
        return pl.BlockSpec((1, TILE, width), lambda b, i: (b, i, 0))

    def const(shape):
        return pl.BlockSpec(shape, lambda b, i: (0,) * len(shape))

    prev_spec = pl.BlockSpec((1, HALO, D_MODEL), lambda b, i: (b, jnp.maximum(i * per - 1, 0), 0))
    next_spec = pl.BlockSpec((1, HALO, D_MODEL),
                             lambda b, i: (b, jnp.minimum((i + 1) * per, n_halo - 1), 0))
    return pl.pallas_call(
        functools.partial(_ffn_kernel, seg_first=seg_first, seg_last=seg_last),
        grid=(BATCH, n_tiles),
        in_specs=[prev_spec, tok(D_MODEL), next_spec, tok(D_MODEL), _mod_spec(layer, t0),
                  const((D_MODEL, 2 * FFN_DIM)), const((3, 2 * FFN_DIM)), const((1, 2 * FFN_DIM)),
                  const((FFN_DIM, D_MODEL)), const((1, D_MODEL))],
        out_specs=tok(D_MODEL),
        out_shape=jax.ShapeDtypeStruct((BATCH, n_tiles * TILE, D_MODEL), F32),
        compiler_params=pltpu.CompilerParams(
            dimension_semantics=("parallel", "parallel"), vmem_limit_bytes=VMEM_LIMIT),
        name="ffn",
    )(h2, h2, h2, x1, mod, wu, cw, cb, wd, post)


def _rope_tables():
    rows = SEQ // GRID_W
    row = jnp.repeat(jnp.arange(rows), GRID_W).astype(F32)
    col = (jnp.arange(rows * GRID_W) % GRID_W).astype(F32)
    n_freq = HEAD_DIM // 4
    inv_freq = ROPE_THETA ** (-jnp.arange(n_freq, dtype=F32) / n_freq)
    ang_r = row[:, None] * inv_freq[None, :]
    ang_c = col[:, None] * inv_freq[None, :]
    ang = jnp.concatenate([ang_r, ang_r, ang_c, ang_c], axis=-1)
    cos = jnp.tile(jnp.cos(ang), (1, LANES // HEAD_DIM))
    sin = jnp.tile(jnp.sin(ang), (1, LANES // HEAD_DIM))
    upper = (jnp.arange(LANES) % 32) < 16
    sin_up = jnp.where(upper[None, :], -sin, 0.0)
    sin_dn = jnp.where(upper[None, :], 0.0, sin)
    pad = jnp.zeros((CTX_LEN, LANES), F32)
    return (jnp.concatenate([jnp.ones((CTX_LEN, LANES), F32), cos], axis=0),
            jnp.concatenate([pad, sin_up], axis=0), jnp.concatenate([pad, sin_dn], axis=0))


def _permute_heads(w, axis):
    idx = jnp.concatenate([jnp.arange(h * HEAD_DIM, (h + 1) * HEAD_DIM) for h in HEAD_PERM])
    return jnp.take(w, idx, axis=axis)


def _layout_w_in(w):
    sizes = (GLA_QK, GLA_QK, GLA_WIDTH, GLA_WIDTH, 2 * GLA_GATE_RANK,
             Q_WIDTH, KV_WIDTH, KV_WIDTH, Q_WIDTH, KV_WIDTH, KV_WIDTH)
    parts, start = [], 0
    for size in sizes:
        parts.append(w[:, start:start + size])
        start += size
    a_q, a_k, a_v, a_r, a_z, b_q, b_k, b_v, c_q, c_k, c_v = parts
    z_pad = jnp.zeros((w.shape[0], LANES - 2 * GLA_GATE_RANK), w.dtype)
    return jnp.concatenate(
        [a_q, a_k, a_v, a_r, _permute_heads(b_q, 1), b_k, b_v, _permute_heads(c_q, 1), c_k, c_v,
         a_z, z_pad], axis=1).astype(BF16)


def _layout_w_out(w):
    a = w[:GLA_WIDTH]
    b = _permute_heads(w[GLA_WIDTH:GLA_WIDTH + Q_WIDTH], 0)
    c = _permute_heads(w[GLA_WIDTH + Q_WIDTH:], 0)
    return jnp.concatenate([a, b, c], axis=0).astype(BF16)


def _layout_gate(w_gate):
    out = jnp.zeros((2, LANES, GLA_QK), w_gate.dtype)
    for d in range(2):
        out = out.at[d, d * GLA_GATE_RANK:(d + 1) * GLA_GATE_RANK].set(w_gate[d])
    return out.astype(BF16)


def _diag_reduce_matrix():
    r = jnp.arange(GLA_SUB * LANES)
    c = jnp.arange(GLA_HEADS * GLA_SUB)
    same_s = (r[:, None] // LANES) == (c[None, :] % GLA_SUB)
    same_h = ((r[:, None] % LANES) // GLA_DK) == (c[None, :] // GLA_SUB)
    return jnp.where(same_s & same_h, 1.0, 0.0).astype(BF16)


def kernel(x, c, ctx, c_ctx, w_mod, b_mod, attn_pre_norm, attn_post_norm, ffn_pre_norm,
           ffn_post_norm, w_in, gla_w_gate, gla_b_gate, gla_out_norm, swa_sink, gqa_q_norm,
           gqa_k_norm, w_out, ffn_w_up, ffn_conv_w, ffn_conv_b, ffn_w_down):
    assert x.shape == (BATCH, SEQ, D_MODEL) and ctx.shape == (BATCH, CTX_LEN, D_MODEL)
    cos, sin_up, sin_dn = _rope_tables()
    wst = _diag_reduce_matrix()
    cs = jnp.concatenate([c, c_ctx[None, :], jnp.zeros((16 - BATCH - 1, D_MODEL), F32)], axis=0)
    mod = _modulation(cs, w_mod, b_mod).reshape(DEPTH, 16, MOD_CHUNKS, D_MODEL)

    x_all = jnp.concatenate([ctx, x], axis=1)
    for layer in range(DEPTH):
        with_ctx = layer < DEPTH - 1
        row = lambda v: v[layer][None, :]
        tile2 = lambda v: jnp.tile(v[layer], LANES // HEAD_DIM)[None, :]
        gq, gk, gv, gr, gz, bq, bk, bv, cq, ck, cv = _in_proj(
            x_all, mod, layer, row(attn_pre_norm), _layout_w_in(w_in[layer]), cos, sin_up, sin_dn,
            tile2(gqa_q_norm), tile2(gqa_k_norm))
        o_f, o_b = _gla(gq, gk, gv, gz, _layout_gate(gla_w_gate[layer]),
                        gla_b_gate[layer][:, None, :], wst)
        b_att = _swa(swa_sink[layer].reshape(KV_HEADS, GROUP), bq, bk, bv, with_ctx)
        c_att = _gqa(cq, ck, cv, with_ctx)
        x1, h2 = _out_proj(x_all, o_f, o_b, gr, b_att, c_att, mod, layer,
                           _layout_w_out(w_out[layer]), tile2(gla_out_norm),
                           row(attn_post_norm), row(ffn_pre_norm), with_ctx)
        x_all = _ffn(h2, x1, mod, layer, ffn_w_up[layer].astype(BF16), ffn_conv_w[layer],
                     row(ffn_conv_b), ffn_w_down[layer].astype(BF16), row(ffn_post_norm), with_ctx)
    return x_all
```

```python
import functools
import itertools
from typing import NamedTuple

import jax
import jax.numpy as jnp
from jax import lax
from jax.experimental import pallas as pl
from jax.experimental.pallas import tpu as pltpu

F32 = jnp.float32
BF16 = jnp.bfloat16

D_MODEL = 1024
BATCH = 8
SEQ = 2048
DEPTH = 2
CTX_LEN = 256
TOK = SEQ + CTX_LEN
GRID_W = 64
HEAD_DIM = 64
ROPE_THETA = 10000.0
NORM_EPS = 1e-6
MOD_CHUNKS = 6

GLA_HEADS = 4
GLA_DK = 32
GLA_DV = 64
GLA_GATE_RANK = 16
GLA_GATE_TAU = 16.0
GLA_QK = GLA_HEADS * GLA_DK
GLA_WIDTH = GLA_HEADS * GLA_DV

Q_HEADS = 6
KV_HEADS = 2
GROUP = Q_HEADS // KV_HEADS
Q_WIDTH = Q_HEADS * HEAD_DIM
KV_WIDTH = KV_HEADS * HEAD_DIM
SWA_BLOCK = 128

FFN_DIM = 2816
FFN_CHUNK = 256

LANES = 128
PROJ_TILE = 512
ROW_PART = 256
FFN_TILE = 512
FFN_PART = 256
HALO = 16
GLA_BLOCK = 256
GLA_CHUNK = 64
GLA_SUB = 16
ATT_TQ = 256
VMEM_LIMIT = 56 * 1024 * 1024

C_GQ, C_GK, C_GV, C_GR = 0, 128, 256, 512
C_BQ, C_BK, C_BV = 768, 1152, 1280
C_CQ, C_CK, C_CV = 1408, 1792, 1920
C_GZ = 2048
IN_COLS = 2176
HEAD_PERM = (0, 3, 1, 4, 2, 5)

_NT = (((1,), (1,)), ((), ()))


class _Segment(NamedTuple):
    tile: int
    n_tiles: int
    first: int
    ctx: bool


def _segment(ctx, tile):
    if ctx:
        return _Segment(CTX_LEN, 1, SEQ // CTX_LEN, True)
    return _Segment(tile, SEQ // tile, 0, False)


def _dot(a, b):
    return jnp.dot(a, b, preferred_element_type=F32)


def _dot_nt(a, b):
    return lax.dot_general(a, b, _NT, preferred_element_type=F32)


def _idiv(x, n):
    assert n & (n - 1) == 0
    return lax.shift_right_logical(x, jnp.int32(n.bit_length() - 1))


def _split2(x):
    hi = x.astype(BF16)
    lo = (x - hi.astype(F32)).astype(BF16)
    return hi, lo


def _split3(x):
    h1 = x.astype(BF16)
    r1 = x - h1.astype(F32)
    h2 = r1.astype(BF16)
    h3 = (r1 - h2.astype(F32)).astype(BF16)
    return h1, h2, h3


def _seg_ones(width, seg):
    r = _idiv(lax.broadcasted_iota(jnp.int32, (width, width), 0), seg)
    c = _idiv(lax.broadcasted_iota(jnp.int32, (width, width), 1), seg)
    return jnp.where(r == c, 1.0, 0.0).astype(BF16)


def _seg_sum(x, ones):
    hi, lo = _split2(x)
    return _dot(hi, ones) + _dot(lo, ones)


def _row_rms(x, w):
    ms = jnp.mean(x * x, axis=-1, keepdims=True)
    return x * lax.rsqrt(ms + NORM_EPS) * w


def _head_rms(x, ones, w):
    ss = _seg_sum(x * x, ones)
    return x * lax.rsqrt(ss * (1.0 / HEAD_DIM) + NORM_EPS) * w


def _silu(x):
    return x / (1.0 + jnp.exp(-x))


def _rope(x, cos, sin_up, sin_dn):
    return x * cos + pltpu.roll(x, LANES - 16, 1) * sin_up + pltpu.roll(x, 16, 1) * sin_dn


def _const_spec(shape):
    return pl.BlockSpec(shape, lambda b, i: (0,) * len(shape))


def _params(*semantics):
    return pltpu.CompilerParams(dimension_semantics=semantics, vmem_limit_bytes=VMEM_LIMIT)


def _mod_kernel(c_ref, w_ref, b_ref, o_ref):
    a = _silu(c_ref[...]).astype(BF16)
    o_ref[0] = _dot(a, w_ref[0].astype(BF16)) + b_ref[0]


def _modulation(cs, w_mod, b_mod):
    tn = 1536
    n = MOD_CHUNKS * D_MODEL
    return pl.pallas_call(
        _mod_kernel,
        grid=(DEPTH, n // tn),
        in_specs=[
            pl.BlockSpec((16, D_MODEL), lambda l, j: (0, 0)),
            pl.BlockSpec((1, D_MODEL, tn), lambda l, j: (l, 0, j)),
            pl.BlockSpec((1, 1, tn), lambda l, j: (l, 0, j)),
        ],
        out_specs=pl.BlockSpec((1, 16, tn), lambda l, j: (l, 0, j)),
        out_shape=jax.ShapeDtypeStruct((DEPTH, 16, n), F32),
        compiler_params=_params("arbitrary", "arbitrary"),
        name="modulation",
    )(cs, w_mod, b_mod.reshape(DEPTH, 1, n))


def _mod_spec(layer, seg):
    return pl.BlockSpec((1, 1, MOD_CHUNKS, D_MODEL),
                        lambda b, i: (layer, BATCH if seg.ctx else b, 0, 0))


_PROJ_OUTPUTS = ((GLA_QK, F32), (GLA_QK, F32), (GLA_WIDTH, F32), (GLA_WIDTH, F32), (LANES, F32),
                 (Q_WIDTH, BF16), (KV_WIDTH, BF16), (KV_WIDTH, BF16),
                 (Q_WIDTH, BF16), (KV_WIDTH, BF16), (KV_WIDTH, BF16))


def _interleave(streams):
    for _ in itertools.zip_longest(*streams):
        pass


def _in_proj_kernel(x_ref, mod_ref, nw_ref, w_ref, cos_ref, su_ref, sd_ref, qn_ref, kn_ref, *rest):
    (gq_ref, gk_ref, gv_ref, gr_ref, gz_ref, bq_ref, bk_ref, bv_ref,
     cq_ref, ck_ref, cv_ref) = rest[-len(_PROJ_OUTPUTS):]
    sh = mod_ref[0, 0, 0:1, :]
    sc = mod_ref[0, 0, 1:2, :]
    ones = _seg_ones(LANES, HEAD_DIM)
    att_scale = HEAD_DIM ** -0.5
    att_w = Q_WIDTH + 2 * KV_WIDTH

    def part(rows):
        h = (_row_rms(x_ref[0, rows, :], nw_ref[...]) * (1.0 + sc) + sh).astype(BF16)

        def proj(c0, width):
            return _dot(h, w_ref[:, c0:c0 + width])

        cos, su, sd = cos_ref[rows, :], su_ref[rows, :], sd_ref[rows, :]
        p_a = proj(C_GQ, C_BQ - C_GQ)
        p_b = proj(C_BQ, att_w)
        yield
        gq_ref[0, rows, :] = p_a[:, C_GQ:C_GQ + GLA_QK] * (GLA_DK ** -0.5)
        gk_ref[0, rows, :] = p_a[:, C_GK:C_GK + GLA_QK]
        gv_ref[0, rows, :] = p_a[:, C_GV:C_GV + GLA_WIDTH]
        gr_ref[0, rows, :] = p_a[:, C_GR:C_GR + GLA_WIDTH]
        p_c = proj(C_CQ, att_w)
        yield
        for g in range(GROUP):
            ls = slice(g * LANES, (g + 1) * LANES)
            bq_ref[0, rows, ls] = (_rope(p_b[:, ls], cos, su, sd) * att_scale).astype(BF16)
        bk_ref[0, rows, :] = _rope(p_b[:, Q_WIDTH:Q_WIDTH + KV_WIDTH], cos, su, sd).astype(BF16)
        bv_ref[0, rows, :] = p_b[:, Q_WIDTH + KV_WIDTH:].astype(BF16)
        gz_ref[0, rows, :] = proj(C_GZ, LANES)
        yield
        for g in range(GROUP):
            ls = slice(g * LANES, (g + 1) * LANES)
            cq = _head_rms(p_c[:, ls], ones, qn_ref[...])
            cq_ref[0, rows, ls] = (_rope(cq, cos, su, sd) * att_scale).astype(BF16)
        ck = _head_rms(p_c[:, Q_WIDTH:Q_WIDTH + KV_WIDTH], ones, kn_ref[...])
        ck_ref[0, rows, :] = _rope(ck, cos, su, sd).astype(BF16)
        cv_ref[0, rows, :] = p_c[:, Q_WIDTH + KV_WIDTH:].astype(BF16)
        yield

    tile = x_ref.shape[1]
    _interleave([part(slice(r, r + ROW_PART)) for r in range(0, tile, ROW_PART)])


def _in_proj(x_seg, seg, into, mod, layer, nw, w, cos, su, sd, qn, kn):
    tile, first = seg.tile, seg.first

    def rows(width):
        return pl.BlockSpec((1, tile, width), lambda b, i: (b, first + i, 0))

    def table():
        return pl.BlockSpec((tile, LANES), lambda b, i: (first + i, 0))

    in_specs = [pl.BlockSpec((1, tile, D_MODEL), lambda b, i: (b, i, 0)), _mod_spec(layer, seg),
                _const_spec((1, D_MODEL)), _const_spec((D_MODEL, IN_COLS)),
                table(), table(), table(), _const_spec((1, LANES)), _const_spec((1, LANES))]
    args = [x_seg, mod, nw, w, cos, su, sd, qn, kn]
    aliases = {}
    if into is not None:
        aliases = {len(args) + k: k for k in range(len(into))}
        in_specs += [pl.BlockSpec(memory_space=pl.ANY)] * len(into)
        args += list(into)
    return pl.pallas_call(
        _in_proj_kernel,
        grid=(BATCH, seg.n_tiles),
        in_specs=in_specs,
        out_specs=[rows(wd) for wd, _ in _PROJ_OUTPUTS],
        out_shape=[jax.ShapeDtypeStruct((BATCH, TOK, wd), dt) for wd, dt in _PROJ_OUTPUTS],
        input_output_aliases=aliases,
        compiler_params=_params("parallel", "parallel"),
        name="in_proj_ctx" if seg.ctx else "in_proj",
    )(*args)


def _lane_group_mask(width, group, h):
    lane = lax.broadcasted_iota(jnp.int32, (1, width), 1)
    return _idiv(lane, group) == h


def _gla_direction(q_ref, k_ref, v_ref, z_ref, wg, bg, wst_ref, o_ref, st_ref, b_s, e_s, rev):
    nb, c_len, sb = GLA_BLOCK, GLA_CHUNK, GLA_SUB
    n_chunks, n_sub = nb // c_len, c_len // sb

    g = _dot(z_ref[0].astype(BF16), wg) + bg
    la = (jnp.minimum(g, 0.0) - jnp.log1p(jnp.exp(-jnp.abs(g)))) * (1.0 / GLA_GATE_TAU)

    row = lax.broadcasted_iota(jnp.int32, (nb, nb), 0)
    col = lax.broadcasted_iota(jnp.int32, (nb, nb), 1)
    inside = _idiv(row, c_len) == _idiv(col, c_len)
    tri = jnp.where(inside & ((col >= row) if rev else (col <= row)), 1.0, 0.0).astype(BF16)
    h1, h2, h3 = _split3(la)
    b = _dot(tri, h1) + _dot(tri, h2) + _dot(tri, h3)
    b_s[...] = b
    yield

    q, k, v = q_ref[0], k_ref[0], v_ref[0]

    t_loc = lax.broadcasted_iota(jnp.int32, (sb, 1), 0)

    def diag_block(jb, carry):
        r0 = pl.multiple_of(jb * sb, sb)
        bb = b_s[pl.ds(r0, sb), :]
        qb = q_ref[0, pl.ds(r0, sb), :]
        for s in range(sb):
            bs = b_s[pl.ds(r0 + s, 1), :]
            ks = k_ref[0, pl.ds(r0 + s, 1), :]
            e = qb * ks * jnp.exp(jnp.minimum(bb - bs, 0.0))
            keep = (t_loc <= s) if rev else (t_loc >= s)
            e_s[pl.ds(r0, sb), s * LANES:(s + 1) * LANES] = jnp.where(keep, e, 0.0).astype(BF16)
        return carry

    lax.fori_loop(0, nb // sb, diag_block, 0)
    a_diag = _dot(e_s[...], wst_ref[...])
    yield

    hm_k = [_lane_group_mask(GLA_QK, GLA_DK, h) for h in range(GLA_HEADS)]
    hm_v = [_lane_group_mask(GLA_WIDTH, GLA_DV, h) for h in range(GLA_HEADS)]
    st_row = _idiv(lax.broadcasted_iota(jnp.int32, (GLA_WIDTH, GLA_QK), 0), GLA_DV)
    st_col = _idiv(lax.broadcasted_iota(jnp.int32, (GLA_WIDTH, GLA_QK), 1), GLA_DK)
    st_mask = st_row == st_col
    v_t = jnp.transpose(v).astype(BF16)
    tok_row = _idiv(lax.broadcasted_iota(jnp.int32, (nb, 1), 0), c_len)
    chunk = [slice(c * c_len, (c + 1) * c_len) for c in range(n_chunks)]
    sub = [slice(j * sb, (j + 1) * sb) for j in range(n_sub)]

    q_hat, kv_t, d_row = [], [], []
    for c in range(n_chunks):
        bc = b[chunk[c]]
        end = 0 if rev else c_len - 1
        b_end = bc[end:end + 1]
        q_hat.append((q[chunk[c]] * jnp.exp(bc)).astype(BF16))
        d_row.append(jnp.exp(b_end))
        k_hat = jnp.where(tok_row == c, k * jnp.exp(jnp.minimum(b_end - b, 0.0)), 0.0)
        kv_t.append(jnp.where(st_mask, _dot(v_t, k_hat.astype(BF16)), 0.0))
    yield

    scores = {}
    for c in range(n_chunks):
        bc, qc, kc = b[chunk[c]], q[chunk[c]], k[chunk[c]]
        for j in range(n_sub):
            if not ((j > 0) if rev else (j < n_sub - 1)):
                continue
            edge = j * sb if rev else (j + 1) * sb - 1
            ref_b = bc[edge:edge + 1]
            later = slice(0, j * sb) if rev else slice((j + 1) * sb, c_len)
            q_l = (qc[later] * jnp.exp(bc[later] - ref_b)).astype(BF16)
            k_j = kc[sub[j]] * jnp.exp(ref_b - bc[sub[j]])
            k_tile = jnp.concatenate(
                [jnp.where(hm_k[h], k_j, 0.0) for h in range(GLA_HEADS)], axis=0).astype(BF16)
            scores[(c, j)] = _dot_nt(q_l, k_tile)
        yield

    o_intra = []
    for c in range(n_chunks):
        vc = v[chunk[c]]
        blocks = [None] * n_sub
        for j in range(n_sub):
            v_exp = jnp.concatenate(
                [jnp.where(hm_v[h], vc[sub[j]], 0.0) for h in range(GLA_HEADS)],
                axis=0).astype(BF16)
            ad = a_diag[c * c_len + j * sb:c * c_len + (j + 1) * sb]
            if (c, j) in scores:
                sc = scores[(c, j)]
                p = jnp.concatenate([sc, ad], axis=0) if rev else jnp.concatenate([ad, sc], axis=0)
                first = 0 if rev else j
            else:
                p, first = ad, j
            contrib = _dot(p.astype(BF16), v_exp)
            for r in range(contrib.shape[0] // sb):
                piece = contrib[r * sb:(r + 1) * sb]
                blocks[first + r] = piece if blocks[first + r] is None else blocks[first + r] + piece
        o_intra.append(jnp.concatenate(blocks, axis=0))
        yield

    state = st_ref[...]
    outs = [None] * n_chunks
    for c in (reversed(range(n_chunks)) if rev else range(n_chunks)):
        outs[c] = o_intra[c] + _dot_nt(q_hat[c], state.astype(BF16))
        state = state * d_row[c] + kv_t[c]
        yield
    st_ref[...] = state
    o_ref[0, 0] = jnp.concatenate(outs, axis=0)


def _gla_kernel(qf, kf, vf, zf, qb, kb, vb, zb, wg_ref, bg_ref, wst_ref, of_ref, ob_ref,
                sf_ref, sb_ref, bf_s, bb_s, ef_s, eb_s):
    @pl.when(pl.program_id(1) == 0)
    def _():
        sf_ref[...] = jnp.zeros_like(sf_ref)
        sb_ref[...] = jnp.zeros_like(sb_ref)

    _interleave([
        _gla_direction(qf, kf, vf, zf, wg_ref[0], bg_ref[0], wst_ref, of_ref, sf_ref, bf_s, ef_s, False),
        _gla_direction(qb, kb, vb, zb, wg_ref[1], bg_ref[1], wst_ref, ob_ref, sb_ref, bb_s, eb_s, True)])


def _gla(gq, gk, gv, gz, wg, bg, wst):
    n_lat = SEQ // GLA_BLOCK
    ctx_blk = n_lat

    def fwd(b, s):
        return (b, jnp.where(s == 0, ctx_blk, s - 1), 0)

    def bwd(b, s):
        return (b, jnp.where(s == 0, ctx_blk, n_lat - s), 0)

    def specs(index_map):
        return [pl.BlockSpec((1, GLA_BLOCK, wd), index_map)
                for wd in (GLA_QK, GLA_QK, GLA_WIDTH, LANES)]

    return pl.pallas_call(
        _gla_kernel,
        grid=(BATCH, n_lat + 1),
        in_specs=specs(fwd) + specs(bwd) + [
            _const_spec((2, LANES, GLA_QK)), _const_spec((2, 1, GLA_QK)),
            _const_spec((GLA_SUB * LANES, GLA_HEADS * GLA_SUB))],
        out_specs=[pl.BlockSpec((1, 1, GLA_BLOCK, GLA_WIDTH), lambda b, s: (0,) + fwd(b, s)),
                   pl.BlockSpec((1, 1, GLA_BLOCK, GLA_WIDTH), lambda b, s: (0,) + bwd(b, s))],
        out_shape=[jax.ShapeDtypeStruct((1, BATCH, TOK, GLA_WIDTH), F32)] * 2,
        scratch_shapes=[pltpu.VMEM((GLA_WIDTH, GLA_QK), F32), pltpu.VMEM((GLA_WIDTH, GLA_QK), F32),
                        pltpu.VMEM((GLA_BLOCK, GLA_QK), F32), pltpu.VMEM((GLA_BLOCK, GLA_QK), F32),
                        pltpu.VMEM((GLA_BLOCK, GLA_SUB * LANES), BF16),
                        pltpu.VMEM((GLA_BLOCK, GLA_SUB * LANES), BF16)],
        compiler_params=_params("parallel", "arbitrary"),
        name="gla",
    )(gq, gk, gv, gz, gq, gk, gv, gz, wg, bg, wst)


def _attend(q_ref, keys, values, sink_ref, bias, groups):
    tq = q_ref.shape[1]
    lane = lax.broadcasted_iota(jnp.int32, (1, LANES), 1)
    first = lane < HEAD_DIM
    units = [(h, gs) for gs in groups for h in range(KV_HEADS)]

    def scores(h, gs):
        mine = first if h == 0 else jnp.logical_not(first)
        qz = [jnp.where(mine, q_ref[0, :, g * LANES:(g + 1) * LANES], 0) for g in gs]
        return _dot_nt(qz[0] if len(gs) == 1 else jnp.concatenate(qz, axis=0), keys)

    outs = {}
    s_next = scores(*units[0])
    for idx, (h, gs) in enumerate(units):
        s = s_next
        if idx + 1 < len(units):
            s_next = scores(*units[idx + 1])
        if bias is not None:
            nb = bias.shape[1]
            s = jnp.concatenate([s[:, :nb] + bias[:s.shape[0]], s[:, nb:]], axis=1)
        m = jnp.max(s, axis=-1, keepdims=True)
        if sink_ref is not None:
            sk = [jnp.full((tq, 1), sink_ref[h, g], F32) for g in gs]
            sk = sk[0] if len(gs) == 1 else jnp.concatenate(sk, axis=0)
            m = jnp.maximum(m, sk)
        p = jnp.exp(s - m)
        den = jnp.sum(p, axis=-1, keepdims=True)
        if sink_ref is not None:
            den = den + jnp.exp(sk - m)
        o = _dot(p.astype(BF16), values) / den
        for r, g in enumerate(gs):
            outs[(h, g)] = o[r * tq:(r + 1) * tq]
    return jnp.concatenate(
        [jnp.where(first, outs[(0, g)], outs[(1, g)]) for g in range(GROUP)], axis=1)


_ALL_GROUPS = (tuple(range(GROUP)),)
_EACH_GROUP = tuple((g,) for g in range(GROUP))


def _swa_kernel(sink_ref, band_ref, q_ref, kc_ref, vc_ref, kp_ref, kk_ref, kn_ref, vp_ref, vk_ref,
                vn_ref, o_ref, *, n_lat):
    i = pl.program_id(1)
    w = SWA_BLOCK

    @pl.when(i >= n_lat)
    def _():
        o_ref[0] = _attend(q_ref, kc_ref[0], vc_ref[0], sink_ref, None,
                           _ALL_GROUPS).astype(o_ref.dtype)

    @pl.when(i < n_lat)
    def _():
        keys = jnp.concatenate([kp_ref[0], kk_ref[0], kn_ref[0], kc_ref[0]], axis=0)
        values = jnp.concatenate([vp_ref[0], vk_ref[0], vn_ref[0], vc_ref[0]], axis=0)
        col = lax.broadcasted_iota(jnp.int32, (1, 3 * w), 1)
        gone = ((col < w) & (i == 0)) | ((col >= 2 * w) & (i == n_lat - 1))
        bias = band_ref[...] + jnp.where(gone, -1e30, 0.0)
        o_ref[0] = _attend(q_ref, keys, values, sink_ref, bias, _ALL_GROUPS).astype(o_ref.dtype)


def _swa(sink, bq, bk, bv, with_ctx):
    w = SWA_BLOCK
    n_lat = SEQ // w
    n_steps = (TOK if with_ctx else SEQ) // w

    def lat_block(shift):
        return pl.BlockSpec((1, w, KV_WIDTH),
                            lambda b, i: (b, jnp.clip(i + shift, 0, n_lat - 1), 0))

    t = jnp.arange(GROUP * w)[:, None] % w
    kk = jnp.arange(3 * w)[None, :]
    band = jnp.where((kk - t >= 0) & (kk - t <= 2 * w), 0.0, -1e30).astype(F32)

    ctx_spec = pl.BlockSpec((1, CTX_LEN, KV_WIDTH), lambda b, i: (b, SEQ // CTX_LEN, 0))
    return pl.pallas_call(
        functools.partial(_swa_kernel, n_lat=n_lat),
        grid=(BATCH, n_steps),
        in_specs=[pl.BlockSpec(memory_space=pltpu.SMEM),
                  _const_spec((GROUP * w, 3 * w)),
                  pl.BlockSpec((1, w, Q_WIDTH), lambda b, i: (b, i, 0)),
                  ctx_spec, ctx_spec,
                  lat_block(-1), lat_block(0), lat_block(1),
                  lat_block(-1), lat_block(0), lat_block(1)],
        out_specs=pl.BlockSpec((1, w, Q_WIDTH), lambda b, i: (b, i, 0)),
        out_shape=jax.ShapeDtypeStruct((BATCH, n_steps * w, Q_WIDTH), BF16),
        compiler_params=_params("parallel", "parallel"),
        name="swa",
    )(sink, band, bq, bk, bv, bk, bk, bk, bv, bv, bv)


def _gqa_kernel(q_ref, k_ref, v_ref, o_ref, *, n_lat):
    i = pl.program_id(1)

    @pl.when(i >= n_lat)
    def _():
        o_ref[0] = _attend(q_ref, k_ref[0, SEQ:TOK, :], v_ref[0, SEQ:TOK, :],
                           None, None, _EACH_GROUP).astype(o_ref.dtype)

    @pl.when(i < n_lat)
    def _():
        o_ref[0] = _attend(q_ref, k_ref[0], v_ref[0], None, None, _EACH_GROUP).astype(o_ref.dtype)


def _gqa(cq, ck, cv, with_ctx):
    tq = ATT_TQ
    n_steps = (TOK if with_ctx else SEQ) // tq
    kv_spec = pl.BlockSpec((1, TOK, KV_WIDTH), lambda b, i: (b, 0, 0))
    return pl.pallas_call(
        functools.partial(_gqa_kernel, n_lat=SEQ // tq),
        grid=(BATCH, n_steps),
        in_specs=[pl.BlockSpec((1, tq, Q_WIDTH), lambda b, i: (b, i, 0)), kv_spec, kv_spec],
        out_specs=pl.BlockSpec((1, tq, Q_WIDTH), lambda b, i: (b, i, 0)),
        out_shape=jax.ShapeDtypeStruct((BATCH, n_steps * tq, Q_WIDTH), BF16),
        compiler_params=_params("parallel", "parallel"),
        name="gqa",
    )(cq, ck, cv)


def _out_proj_kernel(x_ref, of_ref, ob_ref, r_ref, b_ref, c_ref, mod_ref, w_ref, on_ref,
                     post_ref, pre_ref, x1_ref, h2_ref):
    ones = _seg_ones(LANES, GLA_DV)
    g1 = mod_ref[0, 0, 2:3, :]
    sh2 = mod_ref[0, 0, 3:4, :]
    sc2 = mod_ref[0, 0, 4:5, :]

    def part(rows):
        mix = _dot(b_ref[0, rows, :], w_ref[GLA_WIDTH:GLA_WIDTH + Q_WIDTH, :])
        mix = mix + _dot(c_ref[0, rows, :], w_ref[GLA_WIDTH + Q_WIDTH:, :])
        ya = []
        for lp in range(GLA_WIDTH // LANES):
            ls = slice(lp * LANES, (lp + 1) * LANES)
            o = of_ref[0, 0, rows, ls] + ob_ref[0, 0, rows, ls]
            ya.append((_head_rms(o, ones, on_ref[...]) * _silu(r_ref[0, rows, ls])).astype(BF16))
        mix = mix + _dot(jnp.concatenate(ya, axis=1), w_ref[0:GLA_WIDTH, :])
        yield
        x1 = x_ref[0, rows, :] + g1 * _row_rms(mix, post_ref[...])
        x1_ref[0, rows, :] = x1
        h2_ref[0, rows, :] = (_row_rms(x1, pre_ref[...]) * (1.0 + sc2) + sh2).astype(BF16)
        yield

    tile = x_ref.shape[1]
    _interleave([part(slice(r, r + ROW_PART)) for r in range(0, tile, ROW_PART)])


def _out_proj(x_seg, seg, o_f, o_b, gr, b_att, c_att, mod, layer, w, on, post, pre):
    tile, first = seg.tile, seg.first

    def own(width):
        return pl.BlockSpec((1, tile, width), lambda b, i: (b, i, 0))

    def combined(width):
        return pl.BlockSpec((1, tile, width), lambda b, i: (b, first + i, 0))

    gla_spec = pl.BlockSpec((1, 1, tile, GLA_WIDTH), lambda b, i: (0, b, first + i, 0))
    rows = seg.n_tiles * tile
    return pl.pallas_call(
        _out_proj_kernel,
        grid=(BATCH, seg.n_tiles),
        in_specs=[own(D_MODEL), gla_spec, gla_spec, combined(GLA_WIDTH),
                  combined(Q_WIDTH), combined(Q_WIDTH), _mod_spec(layer, seg),
                  _const_spec((D_MODEL, D_MODEL)), _const_spec((1, LANES)),
                  _const_spec((1, D_MODEL)), _const_spec((1, D_MODEL))],
        out_specs=[own(D_MODEL), own(D_MODEL)],
        out_shape=[jax.ShapeDtypeStruct((BATCH, rows, D_MODEL), F32),
                   jax.ShapeDtypeStruct((BATCH, rows, D_MODEL), BF16)],
        compiler_params=_params("parallel", "parallel"),
        name="out_proj_ctx" if seg.ctx else "out_proj",
    )(x_seg, o_f, o_b, gr, b_att, c_att, mod, w, on, post, pre)


def _ffn_kernel(hp_ref, h_ref, hn_ref, x1_ref, mod_ref, wu_ref, cw_ref, cb_ref, wd_ref, post_ref,
                o_ref, *, n_tiles):
    i = pl.program_id(1)
    tile = h_ref.shape[1]
    hp = jnp.where(i != 0, hp_ref[0], 0)
    hn = jnp.where(i != n_tiles - 1, hn_ref[0], 0)
    h_ext = jnp.concatenate([hp, h_ref[0], hn], axis=0)
    part = min(tile, FFN_PART)
    win = part + 2 * HALO
    mid = slice(HALO, HALO + part)

    def conv(u, c0):
        cw = cw_ref[:, c0:c0 + FFN_CHUNK]
        prev = pltpu.roll(u, 1, 0)[mid]
        nxt = pltpu.roll(u, win - 1, 0)[mid]
        return cw[0:1] * prev + cw[1:2] * u[mid] + cw[2:3] * nxt + cb_ref[:, c0:c0 + FFN_CHUNK]

    n_fc = FFN_DIM // FFN_CHUNK

    n_parts = tile // part
    cuts = [0] + [HALO + r * part for r in range(1, n_parts)] + [tile + 2 * HALO]
    lhs = [h_ext[cuts[r]:cuts[r + 1]] for r in range(n_parts)]

    def window(us, r):
        lo, hi = r * part, r * part + win
        pieces = []
        for q in range(n_parts):
            a, b = max(lo, cuts[q]), min(hi, cuts[q + 1])
            if a < b:
                pieces.append(us[q][a - cuts[q]:b - cuts[q]])
        return pieces[0] if len(pieces) == 1 else jnp.concatenate(pieces, axis=0)

    def up(fc):
        ca = fc * FFN_CHUNK
        cg = FFN_DIM + ca
        return ([_dot(x, wu_ref[:, ca:ca + FFN_CHUNK]) for x in lhs],
                [_dot(x, wu_ref[:, cg:cg + FFN_CHUNK]) for x in lhs])

    acc = [jnp.zeros((part, D_MODEL), F32) for _ in range(n_parts)]
    ua, ug = up(0)
    for fc in range(n_fc):
        nxt_u = up(fc + 1) if fc + 1 < n_fc else None
        ca = fc * FFN_CHUNK
        for r in range(n_parts):
            act = (_silu(conv(window(ua, r), ca)) * conv(window(ug, r), FFN_DIM + ca)).astype(BF16)
            acc[r] = acc[r] + _dot(act, wd_ref[ca:ca + FFN_CHUNK, :])
        if nxt_u is not None:
            ua, ug = nxt_u
    g2 = mod_ref[0, 0, 5:6, :]
    for r in range(tile // part):
        rows = slice(r * part, (r + 1) * part)
        o_ref[0, rows, :] = x1_ref[0, rows, :] + g2 * _row_rms(acc[r], post_ref[...])


def _ffn(h2, x1, seg, mod, layer, wu, cw, cb, wd, post):
    tile, n_tiles = seg.tile, seg.n_tiles
    per = tile // HALO
    n_halo = n_tiles * per

    def own(width):
        return pl.BlockSpec((1, tile, width), lambda b, i: (b, i, 0))

    prev_spec = pl.BlockSpec((1, HALO, D_MODEL), lambda b, i: (b, jnp.maximum(i * per - 1, 0), 0))
    next_spec = pl.BlockSpec((1, HALO, D_MODEL),
                             lambda b, i: (b, jnp.minimum((i + 1) * per, n_halo - 1), 0))
    return pl.pallas_call(
        functools.partial(_ffn_kernel, n_tiles=n_tiles),
        grid=(BATCH, n_tiles),
        in_specs=[prev_spec, own(D_MODEL), next_spec, own(D_MODEL), _mod_spec(layer, seg),
                  _const_spec((D_MODEL, 2 * FFN_DIM)), _const_spec((3, 2 * FFN_DIM)),
                  _const_spec((1, 2 * FFN_DIM)), _const_spec((FFN_DIM, D_MODEL)),
                  _const_spec((1, D_MODEL))],
        out_specs=own(D_MODEL),
        out_shape=jax.ShapeDtypeStruct((BATCH, n_tiles * tile, D_MODEL), F32),
        compiler_params=pltpu.CompilerParams(
            dimension_semantics=("parallel", "parallel"), vmem_limit_bytes=VMEM_LIMIT,
            ),
        name="ffn_ctx" if seg.ctx else "ffn",
    )(h2, h2, h2, x1, mod, wu, cw, cb, wd, post)


def _rope_tables():
    rows = SEQ // GRID_W
    row = jnp.repeat(jnp.arange(rows), GRID_W).astype(F32)
    col = (jnp.arange(rows * GRID_W) % GRID_W).astype(F32)
    n_freq = HEAD_DIM // 4
    inv_freq = ROPE_THETA ** (-jnp.arange(n_freq, dtype=F32) / n_freq)
    ang_r = row[:, None] * inv_freq[None, :]
    ang_c = col[:, None] * inv_freq[None, :]
    ang = jnp.concatenate([ang_r, ang_r, ang_c, ang_c], axis=-1)
    cos = jnp.tile(jnp.cos(ang), (1, LANES // HEAD_DIM))
    sin = jnp.tile(jnp.sin(ang), (1, LANES // HEAD_DIM))
    upper = (jnp.arange(LANES) % 32) < 16
    sin_up = jnp.where(upper[None, :], -sin, 0.0)
    sin_dn = jnp.where(upper[None, :], 0.0, sin)
    pad = jnp.zeros((CTX_LEN, LANES), F32)
    return (jnp.concatenate([cos, jnp.ones((CTX_LEN, LANES), F32)], axis=0),
            jnp.concatenate([sin_up, pad], axis=0), jnp.concatenate([sin_dn, pad], axis=0))


def _permute_heads(w, axis):
    idx = jnp.concatenate([jnp.arange(h * HEAD_DIM, (h + 1) * HEAD_DIM) for h in HEAD_PERM])
    return jnp.take(w, idx, axis=axis)


def _layout_w_in(w):
    sizes = (GLA_QK, GLA_QK, GLA_WIDTH, GLA_WIDTH, 2 * GLA_GATE_RANK,
             Q_WIDTH, KV_WIDTH, KV_WIDTH, Q_WIDTH, KV_WIDTH, KV_WIDTH)
    parts, start = [], 0
    for size in sizes:
        parts.append(w[:, start:start + size])
        start += size
    a_q, a_k, a_v, a_r, a_z, b_q, b_k, b_v, c_q, c_k, c_v = parts
    z_pad = jnp.zeros((w.shape[0], LANES - 2 * GLA_GATE_RANK), w.dtype)
    return jnp.concatenate(
        [a_q, a_k, a_v, a_r, _permute_heads(b_q, 1), b_k, b_v, _permute_heads(c_q, 1), c_k, c_v,
         a_z, z_pad], axis=1).astype(BF16)


def _layout_w_out(w):
    a = w[:GLA_WIDTH]
    b = _permute_heads(w[GLA_WIDTH:GLA_WIDTH + Q_WIDTH], 0)
    c = _permute_heads(w[GLA_WIDTH + Q_WIDTH:], 0)
    return jnp.concatenate([a, b, c], axis=0).astype(BF16)


def _layout_gate(w_gate):
    out = jnp.zeros((2, LANES, GLA_QK), w_gate.dtype)
    for d in range(2):
        out = out.at[d, d * GLA_GATE_RANK:(d + 1) * GLA_GATE_RANK].set(w_gate[d])
    return out.astype(BF16)


def _diag_reduce_matrix():
    r = jnp.arange(GLA_SUB * LANES)
    c = jnp.arange(GLA_HEADS * GLA_SUB)
    same_s = (r[:, None] // LANES) == (c[None, :] % GLA_SUB)
    same_h = ((r[:, None] % LANES) // GLA_DK) == (c[None, :] // GLA_SUB)
    return jnp.where(same_s & same_h, 1.0, 0.0).astype(BF16)


def kernel(x, c, ctx, c_ctx, w_mod, b_mod, attn_pre_norm, attn_post_norm, ffn_pre_norm,
           ffn_post_norm, w_in, gla_w_gate, gla_b_gate, gla_out_norm, swa_sink, gqa_q_norm,
           gqa_k_norm, w_out, ffn_w_up, ffn_conv_w, ffn_conv_b, ffn_w_down):
    assert x.shape == (BATCH, SEQ, D_MODEL) and ctx.shape == (BATCH, CTX_LEN, D_MODEL)
    cos, sin_up, sin_dn = _rope_tables()
    wst = _diag_reduce_matrix()
    cs = jnp.concatenate([c, c_ctx[None, :], jnp.zeros((16 - BATCH - 1, D_MODEL), F32)], axis=0)
    mod = _modulation(cs, w_mod, b_mod).reshape(DEPTH, 16, MOD_CHUNKS, D_MODEL)

    lat_proj, ctx_seg = _segment(False, PROJ_TILE), _segment(True, CTX_LEN)
    lat_ffn = _segment(False, FFN_TILE)
    x_lat, x_ctx = x, ctx
    for layer in range(DEPTH):
        with_ctx = layer < DEPTH - 1
        row = lambda v: v[layer][None, :]
        tile2 = lambda v: jnp.tile(v[layer], LANES // HEAD_DIM)[None, :]
        proj_args = (mod, layer, row(attn_pre_norm), _layout_w_in(w_in[layer]), cos, sin_up, sin_dn,
                     tile2(gqa_q_norm), tile2(gqa_k_norm))
        proj = _in_proj(x_lat, lat_proj, None, *proj_args)
        gq, gk, gv, gr, gz, bq, bk, bv, cq, ck, cv = _in_proj(x_ctx, ctx_seg, proj, *proj_args)
        o_f, o_b = _gla(gq, gk, gv, gz, _layout_gate(gla_w_gate[layer]),
                        gla_b_gate[layer][:, None, :], wst)
        b_att = _swa(swa_sink[layer].reshape(KV_HEADS, GROUP), bq, bk, bv, with_ctx)
        c_att = _gqa(cq, ck, cv, with_ctx)
        out_args = (o_f, o_b, gr, b_att, c_att, mod, layer, _layout_w_out(w_out[layer]),
                    tile2(gla_out_norm), row(attn_post_norm), row(ffn_pre_norm))
        ffn_args = (mod, layer, ffn_w_up[layer].astype(BF16), ffn_conv_w[layer], row(ffn_conv_b),
                    ffn_w_down[layer].astype(BF16), row(ffn_post_norm))
        x1, h2 = _out_proj(x_lat, lat_proj, *out_args)
        x_lat = _ffn(h2, x1, lat_ffn, *ffn_args)
        if with_ctx:
            x1, h2 = _out_proj(x_ctx, ctx_seg, *out_args)
            x_ctx = _ffn(h2, x1, ctx_seg, *ffn_args)
    return x_lat
```

```python
import functools
import itertools
from typing import NamedTuple

import jax
import jax.numpy as jnp
from jax import lax
from jax.experimental import pallas as pl
from jax.experimental.pallas import tpu as pltpu

F32 = jnp.float32
BF16 = jnp.bfloat16

D_MODEL = 1024
BATCH = 8
SEQ = 2048
DEPTH = 2
CTX_LEN = 256
TOK = SEQ + CTX_LEN
GRID_W = 64
HEAD_DIM = 64
ROPE_THETA = 10000.0
NORM_EPS = 1e-6
MOD_CHUNKS = 6

GLA_HEADS = 4
GLA_DK = 32
GLA_DV = 64
GLA_GATE_RANK = 16
GLA_GATE_TAU = 16.0
GLA_QK = GLA_HEADS * GLA_DK
GLA_WIDTH = GLA_HEADS * GLA_DV

Q_HEADS = 6
KV_HEADS = 2
GROUP = Q_HEADS // KV_HEADS
Q_WIDTH = Q_HEADS * HEAD_DIM
KV_WIDTH = KV_HEADS * HEAD_DIM
SWA_BLOCK = 128

FFN_DIM = 2816
FFN_CHUNK = 256

LANES = 128
PROJ_TILE = 512
ROW_PART = 256
FFN_TILE = 512
FFN_PART = 256
HALO = 16
GLA_BLOCK = 256
GLA_CHUNK = 64
GLA_SUB = 16
GLA_LAG = 3
ATT_TQ = 256
VMEM_LIMIT = 56 * 1024 * 1024

C_GQ, C_GK, C_GV, C_GR = 0, 128, 256, 512
C_BQ, C_CQ = 768, 1152
C_BK, C_BV, C_CK, C_CV = 1536, 1664, 1792, 1920
C_GZ = 2048
IN_COLS = 2176
HEAD_PERM = (0, 3, 1, 4, 2, 5)

_NT = (((1,), (1,)), ((), ()))


class _Segment(NamedTuple):
    tile: int
    n_tiles: int
    first: int
    ctx: bool


def _segment(ctx, tile):
    if ctx:
        return _Segment(CTX_LEN, 1, SEQ // CTX_LEN, True)
    return _Segment(tile, SEQ // tile, 0, False)


def _dot(a, b):
    return jnp.dot(a, b, preferred_element_type=F32)


def _dot_nt(a, b):
    return lax.dot_general(a, b, _NT, preferred_element_type=F32)


def _idiv(x, n):
    assert n & (n - 1) == 0
    return lax.shift_right_logical(x, jnp.int32(n.bit_length() - 1))


def _split2(x):
    hi = x.astype(BF16)
    lo = (x - hi.astype(F32)).astype(BF16)
    return hi, lo


def _split3(x):
    h1 = x.astype(BF16)
    r1 = x - h1.astype(F32)
    h2 = r1.astype(BF16)
    h3 = (r1 - h2.astype(F32)).astype(BF16)
    return h1, h2, h3


def _seg_ones(width, seg):
    r = _idiv(lax.broadcasted_iota(jnp.int32, (width, width), 0), seg)
    c = _idiv(lax.broadcasted_iota(jnp.int32, (width, width), 1), seg)
    return jnp.where(r == c, 1.0, 0.0).astype(BF16)


def _seg_sum(x, ones):
    hi, lo = _split2(x)
    return _dot(hi, ones) + _dot(lo, ones)


def _row_rms(x, w):
    ms = jnp.mean(x * x, axis=-1, keepdims=True)
    return x * lax.rsqrt(ms + NORM_EPS) * w


def _head_rms(x, ones, w):
    ss = _seg_sum(x * x, ones)
    return x * lax.rsqrt(ss * (1.0 / HEAD_DIM) + NORM_EPS) * w


def _silu(x):
    return x / (1.0 + jnp.exp(-x))


def _rope(x, cos, sin_up, sin_dn):
    return x * cos + pltpu.roll(x, LANES - 16, 1) * sin_up + pltpu.roll(x, 16, 1) * sin_dn


def _const_spec(shape):
    return pl.BlockSpec(shape, lambda b, i: (0,) * len(shape))


def _params(*semantics):
    return pltpu.CompilerParams(dimension_semantics=semantics, vmem_limit_bytes=VMEM_LIMIT)


def _mod_kernel(c_ref, w_ref, b_ref, o_ref):
    a = _silu(c_ref[...]).astype(BF16)
    o_ref[0] = _dot(a, w_ref[0].astype(BF16)) + b_ref[0]


def _modulation(cs, w_mod, b_mod):
    tn = 1536
    n = MOD_CHUNKS * D_MODEL
    return pl.pallas_call(
        _mod_kernel,
        grid=(DEPTH, n // tn),
        in_specs=[
            pl.BlockSpec((16, D_MODEL), lambda l, j: (0, 0)),
            pl.BlockSpec((1, D_MODEL, tn), lambda l, j: (l, 0, j)),
            pl.BlockSpec((1, 1, tn), lambda l, j: (l, 0, j)),
        ],
        out_specs=pl.BlockSpec((1, 16, tn), lambda l, j: (l, 0, j)),
        out_shape=jax.ShapeDtypeStruct((DEPTH, 16, n), F32),
        compiler_params=_params("arbitrary", "arbitrary"),
        name="modulation",
    )(cs, w_mod, b_mod.reshape(DEPTH, 1, n))


def _mod_spec(layer, seg):
    return pl.BlockSpec((1, 1, MOD_CHUNKS, D_MODEL),
                        lambda b, i: (layer, BATCH if seg.ctx else b, 0, 0))


_PROJ_OUTPUTS = ((GLA_QK, F32), (GLA_QK, F32), (GLA_WIDTH, F32), (GLA_WIDTH, F32), (LANES, F32),
                 (Q_WIDTH, BF16), (KV_WIDTH, BF16), (KV_WIDTH, BF16),
                 (Q_WIDTH, BF16), (KV_WIDTH, BF16), (KV_WIDTH, BF16))


def _interleave(streams):
    for _ in itertools.zip_longest(*streams):
        pass


def _lag(stream, stages):
    for _ in range(stages):
        yield
    yield from stream


def _in_proj_kernel(x_ref, mod_ref, nw_ref, w_ref, cos_ref, su_ref, sd_ref, qn_ref, kn_ref, *rest):
    (gq_ref, gk_ref, gv_ref, gr_ref, gz_ref, bq_ref, bk_ref, bv_ref,
     cq_ref, ck_ref, cv_ref) = rest[-len(_PROJ_OUTPUTS):]
    sh = mod_ref[0, 0, 0:1, :]
    sc = mod_ref[0, 0, 1:2, :]
    ones = _seg_ones(LANES, HEAD_DIM)
    att_scale = HEAD_DIM ** -0.5

    def part(rows):
        h = (_row_rms(x_ref[0, rows, :], nw_ref[...]) * (1.0 + sc) + sh).astype(BF16)

        def proj(c0, width):
            return _dot(h, w_ref[:, c0:c0 + width])

        cos, su, sd = cos_ref[rows, :], su_ref[rows, :], sd_ref[rows, :]
        p_a = proj(C_GQ, C_BQ - C_GQ)
        p_q = proj(C_BQ, C_BK - C_BQ)
        yield
        gq_ref[0, rows, :] = p_a[:, C_GQ:C_GQ + GLA_QK] * (GLA_DK ** -0.5)
        gk_ref[0, rows, :] = p_a[:, C_GK:C_GK + GLA_QK]
        gv_ref[0, rows, :] = p_a[:, C_GV:C_GV + GLA_WIDTH]
        gr_ref[0, rows, :] = p_a[:, C_GR:C_GR + GLA_WIDTH]
        p_kv = proj(C_BK, IN_COLS - C_BK)

        def kv(c0):
            return p_kv[:, c0 - C_BK:c0 - C_BK + KV_WIDTH]
        yield
        for g in range(GROUP):
            ls = slice(g * LANES, (g + 1) * LANES)
            bq_ref[0, rows, ls] = (_rope(p_q[:, ls], cos, su, sd) * att_scale).astype(BF16)
            cq = _head_rms(p_q[:, Q_WIDTH + g * LANES:Q_WIDTH + (g + 1) * LANES], ones, qn_ref[...])
            cq_ref[0, rows, ls] = (_rope(cq, cos, su, sd) * att_scale).astype(BF16)
        yield
        bk_ref[0, rows, :] = _rope(kv(C_BK), cos, su, sd).astype(BF16)
        bv_ref[0, rows, :] = kv(C_BV).astype(BF16)
        ck_ref[0, rows, :] = _rope(_head_rms(kv(C_CK), ones, kn_ref[...]), cos, su, sd).astype(BF16)
        cv_ref[0, rows, :] = kv(C_CV).astype(BF16)
        gz_ref[0, rows, :] = kv(C_GZ)
        yield

    tile = x_ref.shape[1]
    _interleave([part(slice(r, r + ROW_PART)) for r in range(0, tile, ROW_PART)])


def _in_proj(x_seg, seg, into, mod, layer, nw, w, cos, su, sd, qn, kn):
    tile, first = seg.tile, seg.first

    def rows(width):
        return pl.BlockSpec((1, tile, width), lambda b, i: (b, first + i, 0))

    def table():
        return pl.BlockSpec((tile, LANES), lambda b, i: (first + i, 0))

    in_specs = [pl.BlockSpec((1, tile, D_MODEL), lambda b, i: (b, i, 0)), _mod_spec(layer, seg),
                _const_spec((1, D_MODEL)), _const_spec((D_MODEL, IN_COLS)),
                table(), table(), table(), _const_spec((1, LANES)), _const_spec((1, LANES))]
    args = [x_seg, mod, nw, w, cos, su, sd, qn, kn]
    aliases = {}
    if into is not None:
        aliases = {len(args) + k: k for k in range(len(into))}
        in_specs += [pl.BlockSpec(memory_space=pl.ANY)] * len(into)
        args += list(into)
    return pl.pallas_call(
        _in_proj_kernel,
        grid=(BATCH, seg.n_tiles),
        in_specs=in_specs,
        out_specs=[rows(wd) for wd, _ in _PROJ_OUTPUTS],
        out_shape=[jax.ShapeDtypeStruct((BATCH, TOK, wd), dt) for wd, dt in _PROJ_OUTPUTS],
        input_output_aliases=aliases,
        compiler_params=_params("parallel", "parallel"),
        name="in_proj_ctx" if seg.ctx else "in_proj",
    )(*args)


def _lane_group_mask(width, group, h):
    lane = lax.broadcasted_iota(jnp.int32, (1, width), 1)
    return _idiv(lane, group) == h


def _gla_direction(q_ref, k_ref, v_ref, z_ref, wg, bg, tri, wst_ref, o_ref, st_ref, b_s, e_s, rev):
    nb, c_len, sb = GLA_BLOCK, GLA_CHUNK, GLA_SUB
    n_chunks, n_sub = nb // c_len, c_len // sb

    g = _dot(z_ref[0].astype(BF16), wg) + bg
    la = (jnp.minimum(g, 0.0) - jnp.log1p(jnp.exp(-jnp.abs(g)))) * (1.0 / GLA_GATE_TAU)

    h1, h2, h3 = _split3(la)
    b = _dot(tri, h1) + _dot(tri, h2) + _dot(tri, h3)
    b_s[...] = b
    yield

    q, k, v = q_ref[0], k_ref[0], v_ref[0]

    half = sb // 2
    t_half = lax.broadcasted_iota(jnp.int32, (half, 1), 0)

    for jb in range(nb // sb):
        r0 = jb * sb
        bb = [b_s[pl.ds(r0 + lo, half), :] for lo in (0, half)]
        qb = [q_ref[0, pl.ds(r0 + lo, half), :] for lo in (0, half)]
        for s in range(sb):
            bs = b_s[pl.ds(r0 + s, 1), :]
            ks = k_ref[0, pl.ds(r0 + s, 1), :]
            halves = []
            for hi, lo in enumerate((0, half)):
                none = (lo > s) if rev else (lo + half - 1 < s)
                every = (lo + half - 1 <= s) if rev else (lo >= s)
                if none:
                    halves.append(jnp.zeros((half, LANES), F32))
                    continue
                diff = bb[hi] - bs
                if every:
                    e = qb[hi] * ks * jnp.exp(diff)
                else:
                    keep = (t_half + lo <= s) if rev else (t_half + lo >= s)
                    e = jnp.where(keep, qb[hi] * ks * jnp.exp(jnp.minimum(diff, 0.0)), 0.0)
                halves.append(e)
            e_s[pl.ds(r0, sb), s * LANES:(s + 1) * LANES] = jnp.concatenate(
                halves, axis=0).astype(BF16)
        if jb % 4 == 3:
            yield

    a_diag = _dot(e_s[...], wst_ref[...])
    yield

    hm_k = [_lane_group_mask(GLA_QK, GLA_DK, h) for h in range(GLA_HEADS)]
    hm_v = [_lane_group_mask(GLA_WIDTH, GLA_DV, h) for h in range(GLA_HEADS)]
    st_row = _idiv(lax.broadcasted_iota(jnp.int32, (GLA_WIDTH, GLA_QK), 0), GLA_DV)
    st_col = _idiv(lax.broadcasted_iota(jnp.int32, (GLA_WIDTH, GLA_QK), 1), GLA_DK)
    st_mask = st_row == st_col
    v_t = jnp.transpose(v).astype(BF16)
    chunk = [slice(c * c_len, (c + 1) * c_len) for c in range(n_chunks)]
    sub = [slice(j * sb, (j + 1) * sb) for j in range(n_sub)]

    q_hat, kv_t, d_row = [], [], []
    for c in range(n_chunks):
        bc = b[chunk[c]]
        end = 0 if rev else c_len - 1
        b_end = bc[end:end + 1]
        q_hat.append((q[chunk[c]] * jnp.exp(bc)).astype(BF16))
        d_row.append(jnp.exp(b_end))
        k_hat = (k[chunk[c]] * jnp.exp(b_end - bc)).astype(BF16)
        pads = [jnp.zeros((n, GLA_QK), BF16) for n in (c * c_len, nb - (c + 1) * c_len) if n]
        k_hat = jnp.concatenate(pads[:1 if c else 0] + [k_hat] + pads[1 if c else 0:], axis=0)
        kv_t.append(jnp.where(st_mask, _dot(v_t, k_hat), 0.0))
    yield

    scores = {}
    for c in range(n_chunks):
        bc, qc, kc = b[chunk[c]], q[chunk[c]], k[chunk[c]]
        for j in range(n_sub):
            if not ((j > 0) if rev else (j < n_sub - 1)):
                continue
            edge = j * sb if rev else (j + 1) * sb - 1
            ref_b = bc[edge:edge + 1]
            later = slice(0, j * sb) if rev else slice((j + 1) * sb, c_len)
            q_l = (qc[later] * jnp.exp(bc[later] - ref_b)).astype(BF16)
            k_j = kc[sub[j]] * jnp.exp(ref_b - bc[sub[j]])
            k_tile = jnp.concatenate(
                [jnp.where(hm_k[h], k_j, 0.0) for h in range(GLA_HEADS)], axis=0).astype(BF16)
            scores[(c, j)] = _dot_nt(q_l, k_tile)
        yield

    o_intra = []
    for c in range(n_chunks):
        vc = v[chunk[c]]
        blocks = [None] * n_sub
        for j in range(n_sub):
            v_exp = jnp.concatenate(
                [jnp.where(hm_v[h], vc[sub[j]], 0.0) for h in range(GLA_HEADS)],
                axis=0).astype(BF16)
            ad = a_diag[c * c_len + j * sb:c * c_len + (j + 1) * sb]
            if (c, j) in scores:
                sc = scores[(c, j)]
                p = jnp.concatenate([sc, ad], axis=0) if rev else jnp.concatenate([ad, sc], axis=0)
                first = 0 if rev else j
            else:
                p, first = ad, j
            contrib = _dot(p.astype(BF16), v_exp)
            for r in range(contrib.shape[0] // sb):
                piece = contrib[r * sb:(r + 1) * sb]
                blocks[first + r] = piece if blocks[first + r] is None else blocks[first + r] + piece
        o_intra.append(jnp.concatenate(blocks, axis=0))
        yield

    state = st_ref[...]
    outs = [None] * n_chunks
    for c in (reversed(range(n_chunks)) if rev else range(n_chunks)):
        outs[c] = o_intra[c] + _dot_nt(q_hat[c], state.astype(BF16))
        state = state * d_row[c] + kv_t[c]
        yield
    st_ref[...] = state
    o_ref[0, 0] = jnp.concatenate(outs, axis=0)


def _gla_kernel(qf, kf, vf, zf, qb, kb, vb, zb, wg_ref, bg_ref, tri_ref, wst_ref, of_ref, ob_ref,
                sf_ref, sb_ref, bf_s, bb_s, ef_s, eb_s):
    @pl.when(pl.program_id(1) == 0)
    def _():
        sf_ref[...] = jnp.zeros_like(sf_ref)
        sb_ref[...] = jnp.zeros_like(sb_ref)

    _interleave([
        _gla_direction(qf, kf, vf, zf, wg_ref[0], bg_ref[0], tri_ref[0], wst_ref, of_ref, sf_ref,
                       bf_s, ef_s, False),
        _lag(_gla_direction(qb, kb, vb, zb, wg_ref[1], bg_ref[1], tri_ref[1], wst_ref, ob_ref, sb_ref,
                            bb_s, eb_s, True), GLA_LAG)])


def _cumsum_matrices():
    t = jnp.arange(GLA_BLOCK)[:, None]
    s = jnp.arange(GLA_BLOCK)[None, :]
    inside = (t // GLA_CHUNK) == (s // GLA_CHUNK)
    return jnp.stack([jnp.where(inside & (s <= t), 1.0, 0.0),
                      jnp.where(inside & (s >= t), 1.0, 0.0)]).astype(BF16)


def _gla(gq, gk, gv, gz, wg, bg, tri, wst):
    n_lat = SEQ // GLA_BLOCK
    ctx_blk = n_lat

    def fwd(b, s):
        return (b, jnp.where(s == 0, ctx_blk, s - 1), 0)

    def bwd(b, s):
        return (b, jnp.where(s == 0, ctx_blk, n_lat - s), 0)

    def specs(index_map):
        return [pl.BlockSpec((1, GLA_BLOCK, wd), index_map)
                for wd in (GLA_QK, GLA_QK, GLA_WIDTH, LANES)]

    return pl.pallas_call(
        _gla_kernel,
        grid=(BATCH, n_lat + 1),
        in_specs=specs(fwd) + specs(bwd) + [
            _const_spec((2, LANES, GLA_QK)), _const_spec((2, 1, GLA_QK)),
            _const_spec((2, GLA_BLOCK, GLA_BLOCK)),
            _const_spec((GLA_SUB * LANES, GLA_HEADS * GLA_SUB))],
        out_specs=[pl.BlockSpec((1, 1, GLA_BLOCK, GLA_WIDTH), lambda b, s: (0,) + fwd(b, s)),
                   pl.BlockSpec((1, 1, GLA_BLOCK, GLA_WIDTH), lambda b, s: (0,) + bwd(b, s))],
        out_shape=[jax.ShapeDtypeStruct((1, BATCH, TOK, GLA_WIDTH), F32)] * 2,
        scratch_shapes=[pltpu.VMEM((GLA_WIDTH, GLA_QK), F32), pltpu.VMEM((GLA_WIDTH, GLA_QK), F32),
                        pltpu.VMEM((GLA_BLOCK, GLA_QK), F32), pltpu.VMEM((GLA_BLOCK, GLA_QK), F32),
                        pltpu.VMEM((GLA_BLOCK, GLA_SUB * LANES), BF16),
                        pltpu.VMEM((GLA_BLOCK, GLA_SUB * LANES), BF16)],
        compiler_params=_params("parallel", "arbitrary"),
        name="gla",
    )(gq, gk, gv, gz, gq, gk, gv, gz, wg, bg, tri, wst)


def _fold_lanes(x, op):
    acc = x[:, 0:LANES]
    for c in range(1, x.shape[1] // LANES):
        acc = op(acc, x[:, c * LANES:(c + 1) * LANES])
    return acc


class _Job(NamedTuple):
    rows: slice
    keys: jax.Array
    values: jax.Array
    bias: jax.Array


def _attend(q_ref, jobs, sink_ref, groups, ahead):
    lane = lax.broadcasted_iota(jnp.int32, (1, LANES), 1)
    first = lane < HEAD_DIM
    units = [(jb, h, gs) for jb in range(len(jobs)) for gs in groups for h in range(KV_HEADS)]

    def scores(jb, h, gs):
        mine = first if h == 0 else jnp.logical_not(first)
        qz = [jnp.where(mine, q_ref[0, jobs[jb].rows, g * LANES:(g + 1) * LANES], 0) for g in gs]
        qz = qz[0] if len(gs) == 1 else jnp.concatenate(qz, axis=0)
        return _dot_nt(qz, jobs[jb].keys)

    pending = {}
    for idx in range(min(ahead, len(units))):
        pending[idx] = scores(*units[idx])
    outs = {}
    for idx, (jb, h, gs) in enumerate(units):
        if idx + ahead < len(units):
            pending[idx + ahead] = scores(*units[idx + ahead])
        s = pending.pop(idx)
        job = jobs[jb]
        tq = job.rows.stop - job.rows.start
        if job.bias is not None:
            nb = job.bias.shape[1]
            s = jnp.concatenate([s[:, :nb] + job.bias[:s.shape[0]], s[:, nb:]], axis=1)
        m = jnp.max(_fold_lanes(s, jnp.maximum), axis=-1, keepdims=True)
        if sink_ref is not None:
            sk = [jnp.full((tq, 1), sink_ref[h, g], F32) for g in gs]
            sk = sk[0] if len(gs) == 1 else jnp.concatenate(sk, axis=0)
            m = jnp.maximum(m, sk)
        p = jnp.exp(s - m)
        den = jnp.sum(_fold_lanes(p, jnp.add), axis=-1, keepdims=True)
        if sink_ref is not None:
            den = den + jnp.exp(sk - m)
        o = _dot(p.astype(BF16), job.values) / den
        for r, g in enumerate(gs):
            outs[(jb, h, g)] = o[r * tq:(r + 1) * tq]
    return [jnp.concatenate([jnp.where(first, outs[(jb, 0, g)], outs[(jb, 1, g)])
                             for g in range(GROUP)], axis=1) for jb in range(len(jobs))]


_ALL_GROUPS = (tuple(range(GROUP)),)
_EACH_GROUP = tuple((g,) for g in range(GROUP))


def _swa_kernel(sink_ref, band_ref, q_ref, kc_ref, vc_ref, kp_ref, kk_ref, kn_ref, vp_ref, vk_ref,
                vn_ref, o_ref, *, n_lat):
    i = pl.program_id(1)
    w = SWA_BLOCK

    @pl.when(i >= n_lat)
    def _():
        job = _Job(slice(0, 2 * w), kc_ref[0], vc_ref[0], None)
        o_ref[0] = _attend(q_ref, [job], sink_ref, _ALL_GROUPS, 2)[0].astype(o_ref.dtype)

    @pl.when(i < n_lat)
    def _():
        k_lo, k_hi = kk_ref[0, 0:w, :], kk_ref[0, w:2 * w, :]
        v_lo, v_hi = vk_ref[0, 0:w, :], vk_ref[0, w:2 * w, :]
        col = lax.broadcasted_iota(jnp.int32, (1, 3 * w), 1)
        no_prev = jnp.where((col < w) & (i == 0), -1e30, 0.0)
        no_next = jnp.where((col >= 2 * w) & (i == n_lat - 1), -1e30, 0.0)
        jobs = [
            _Job(slice(0, w), jnp.concatenate([kp_ref[0], k_lo, k_hi, kc_ref[0]], axis=0),
                 jnp.concatenate([vp_ref[0], v_lo, v_hi, vc_ref[0]], axis=0),
                 band_ref[...] + no_prev),
            _Job(slice(w, 2 * w), jnp.concatenate([k_lo, k_hi, kn_ref[0], kc_ref[0]], axis=0),
                 jnp.concatenate([v_lo, v_hi, vn_ref[0], vc_ref[0]], axis=0),
                 band_ref[...] + no_next)]
        outs = _attend(q_ref, jobs, sink_ref, _ALL_GROUPS, 4)
        o_ref[0] = jnp.concatenate(outs, axis=0).astype(o_ref.dtype)


def _swa(sink, bq, bk, bv, with_ctx):
    w = SWA_BLOCK
    n_blocks = SEQ // w
    n_lat = n_blocks // 2
    n_steps = (TOK if with_ctx else SEQ) // (2 * w)

    def edge_block(which):
        return pl.BlockSpec(
            (1, w, KV_WIDTH),
            lambda b, i: (b, jnp.clip(2 * i - 1 if which < 0 else 2 * i + 2, 0, n_blocks - 1), 0))

    own_spec = pl.BlockSpec((1, 2 * w, KV_WIDTH), lambda b, i: (b, jnp.minimum(i, n_lat - 1), 0))

    t = jnp.arange(GROUP * w)[:, None] % w
    kk = jnp.arange(3 * w)[None, :]
    band = jnp.where((kk - t >= 0) & (kk - t <= 2 * w), 0.0, -1e30).astype(F32)

    ctx_spec = pl.BlockSpec((1, CTX_LEN, KV_WIDTH), lambda b, i: (b, SEQ // CTX_LEN, 0))
    return pl.pallas_call(
        functools.partial(_swa_kernel, n_lat=n_lat),
        grid=(BATCH, n_steps),
        in_specs=[pl.BlockSpec(memory_space=pltpu.SMEM),
                  _const_spec((GROUP * w, 3 * w)),
                  pl.BlockSpec((1, 2 * w, Q_WIDTH), lambda b, i: (b, i, 0)),
                  ctx_spec, ctx_spec,
                  edge_block(-1), own_spec, edge_block(1),
                  edge_block(-1), own_spec, edge_block(1)],
        out_specs=pl.BlockSpec((1, 2 * w, Q_WIDTH), lambda b, i: (b, i, 0)),
        out_shape=jax.ShapeDtypeStruct((BATCH, n_steps * 2 * w, Q_WIDTH), BF16),
        compiler_params=_params("parallel", "parallel"),
        name="swa",
    )(sink, band, bq, bk, bv, bk, bk, bk, bv, bv, bv)


def _gqa_kernel(q_ref, k_ref, v_ref, o_ref, *, n_lat):
    i = pl.program_id(1)

    @pl.when(i >= n_lat)
    def _():
        job = _Job(slice(0, ATT_TQ), k_ref[0, SEQ:TOK, :], v_ref[0, SEQ:TOK, :], None)
        o_ref[0] = _attend(q_ref, [job], None, _EACH_GROUP, 1)[0].astype(o_ref.dtype)

    @pl.when(i < n_lat)
    def _():
        job = _Job(slice(0, ATT_TQ), k_ref[0], v_ref[0], None)
        o_ref[0] = _attend(q_ref, [job], None, _EACH_GROUP, 1)[0].astype(o_ref.dtype)


def _gqa(cq, ck, cv, with_ctx):
    tq = ATT_TQ
    n_steps = (TOK if with_ctx else SEQ) // tq
    kv_spec = pl.BlockSpec((1, TOK, KV_WIDTH), lambda b, i: (b, 0, 0))
    return pl.pallas_call(
        functools.partial(_gqa_kernel, n_lat=SEQ // tq),
        grid=(BATCH, n_steps),
        in_specs=[pl.BlockSpec((1, tq, Q_WIDTH), lambda b, i: (b, i, 0)), kv_spec, kv_spec],
        out_specs=pl.BlockSpec((1, tq, Q_WIDTH), lambda b, i: (b, i, 0)),
        out_shape=jax.ShapeDtypeStruct((BATCH, n_steps * tq, Q_WIDTH), BF16),
        compiler_params=_params("parallel", "parallel"),
        name="gqa",
    )(cq, ck, cv)


def _out_proj_kernel(x_ref, of_ref, ob_ref, r_ref, b_ref, c_ref, mod_ref, w_ref, on_ref,
                     post_ref, pre_ref, x1_ref, h2_ref):
    ones = _seg_ones(LANES, GLA_DV)
    g1 = mod_ref[0, 0, 2:3, :]
    sh2 = mod_ref[0, 0, 3:4, :]
    sc2 = mod_ref[0, 0, 4:5, :]

    def part(rows):
        mix = _dot(b_ref[0, rows, :], w_ref[GLA_WIDTH:GLA_WIDTH + Q_WIDTH, :])
        mix = mix + _dot(c_ref[0, rows, :], w_ref[GLA_WIDTH + Q_WIDTH:, :])
        ya = []
        for lp in range(GLA_WIDTH // LANES):
            ls = slice(lp * LANES, (lp + 1) * LANES)
            o = of_ref[0, 0, rows, ls] + ob_ref[0, 0, rows, ls]
            ya.append((_head_rms(o, ones, on_ref[...]) * _silu(r_ref[0, rows, ls])).astype(BF16))
        mix = mix + _dot(jnp.concatenate(ya, axis=1), w_ref[0:GLA_WIDTH, :])
        yield
        x1 = x_ref[0, rows, :] + g1 * _row_rms(mix, post_ref[...])
        x1_ref[0, rows, :] = x1
        h2_ref[0, rows, :] = (_row_rms(x1, pre_ref[...]) * (1.0 + sc2) + sh2).astype(BF16)
        yield

    tile = x_ref.shape[1]
    _interleave([part(slice(r, r + ROW_PART)) for r in range(0, tile, ROW_PART)])


def _out_proj(x_seg, seg, o_f, o_b, gr, b_att, c_att, mod, layer, w, on, post, pre):
    tile, first = seg.tile, seg.first

    def own(width):
        return pl.BlockSpec((1, tile, width), lambda b, i: (b, i, 0))

    def combined(width):
        return pl.BlockSpec((1, tile, width), lambda b, i: (b, first + i, 0))

    gla_spec = pl.BlockSpec((1, 1, tile, GLA_WIDTH), lambda b, i: (0, b, first + i, 0))
    rows = seg.n_tiles * tile
    return pl.pallas_call(
        _out_proj_kernel,
        grid=(BATCH, seg.n_tiles),
        in_specs=[own(D_MODEL), gla_spec, gla_spec, combined(GLA_WIDTH),
                  combined(Q_WIDTH), combined(Q_WIDTH), _mod_spec(layer, seg),
                  _const_spec((D_MODEL, D_MODEL)), _const_spec((1, LANES)),
                  _const_spec((1, D_MODEL)), _const_spec((1, D_MODEL))],
        out_specs=[own(D_MODEL), own(D_MODEL)],
        out_shape=[jax.ShapeDtypeStruct((BATCH, rows, D_MODEL), F32),
                   jax.ShapeDtypeStruct((BATCH, rows, D_MODEL), BF16)],
        compiler_params=_params("parallel", "parallel"),
        name="out_proj_ctx" if seg.ctx else "out_proj",
    )(x_seg, o_f, o_b, gr, b_att, c_att, mod, w, on, post, pre)


def _ffn_kernel(hp_ref, h_ref, hn_ref, x1_ref, mod_ref, wu_ref, cw_ref, cb_ref, wd_ref, post_ref,
                o_ref, *, n_tiles):
    i = pl.program_id(1)
    tile = h_ref.shape[1]
    hp = jnp.where(i != 0, hp_ref[0], 0)
    hn = jnp.where(i != n_tiles - 1, hn_ref[0], 0)
    h_ext = jnp.concatenate([hp, h_ref[0], hn], axis=0)
    part = min(tile, FFN_PART)
    win = part + 2 * HALO
    mid = slice(HALO, HALO + part)

    def conv(u, c0):
        cw = cw_ref[:, c0:c0 + FFN_CHUNK]
        prev = pltpu.roll(u, 1, 0)[mid]
        nxt = pltpu.roll(u, win - 1, 0)[mid]
        return cw[0:1] * prev + cw[1:2] * u[mid] + cw[2:3] * nxt + cb_ref[:, c0:c0 + FFN_CHUNK]

    n_fc = FFN_DIM // FFN_CHUNK

    n_parts = tile // part
    cuts = [0] + [HALO + r * part for r in range(1, n_parts)] + [tile + 2 * HALO]
    lhs = [h_ext[cuts[r]:cuts[r + 1]] for r in range(n_parts)]

    def window(us, r):
        lo, hi = r * part, r * part + win
        pieces = []
        for q in range(n_parts):
            a, b = max(lo, cuts[q]), min(hi, cuts[q + 1])
            if a < b:
                pieces.append(us[q][a - cuts[q]:b - cuts[q]])
        return pieces[0] if len(pieces) == 1 else jnp.concatenate(pieces, axis=0)

    def up(fc):
        ca = fc * FFN_CHUNK
        cg = FFN_DIM + ca
        return ([_dot(x, wu_ref[:, ca:ca + FFN_CHUNK]) for x in lhs],
                [_dot(x, wu_ref[:, cg:cg + FFN_CHUNK]) for x in lhs])

    acc = [jnp.zeros((part, D_MODEL), F32) for _ in range(n_parts)]
    ua, ug = up(0)
    for fc in range(n_fc):
        nxt_u = up(fc + 1) if fc + 1 < n_fc else None
        ca = fc * FFN_CHUNK
        for r in range(n_parts):
            act = (_silu(conv(window(ua, r), ca)) * conv(window(ug, r), FFN_DIM + ca)).astype(BF16)
            acc[r] = acc[r] + _dot(act, wd_ref[ca:ca + FFN_CHUNK, :])
        if nxt_u is not None:
            ua, ug = nxt_u
    g2 = mod_ref[0, 0, 5:6, :]
    for r in range(tile // part):
        rows = slice(r * part, (r + 1) * part)
        o_ref[0, rows, :] = x1_ref[0, rows, :] + g2 * _row_rms(acc[r], post_ref[...])


def _ffn(h2, x1, seg, mod, layer, wu, cw, cb, wd, post):
    tile, n_tiles = seg.tile, seg.n_tiles
    per = tile // HALO
    n_halo = n_tiles * per

    def own(width):
        return pl.BlockSpec((1, tile, width), lambda b, i: (b, i, 0))

    prev_spec = pl.BlockSpec((1, HALO, D_MODEL), lambda b, i: (b, jnp.maximum(i * per - 1, 0), 0))
    next_spec = pl.BlockSpec((1, HALO, D_MODEL),
                             lambda b, i: (b, jnp.minimum((i + 1) * per, n_halo - 1), 0))
    return pl.pallas_call(
        functools.partial(_ffn_kernel, n_tiles=n_tiles),
        grid=(BATCH, n_tiles),
        in_specs=[prev_spec, own(D_MODEL), next_spec, own(D_MODEL), _mod_spec(layer, seg),
                  _const_spec((D_MODEL, 2 * FFN_DIM)), _const_spec((3, 2 * FFN_DIM)),
                  _const_spec((1, 2 * FFN_DIM)), _const_spec((FFN_DIM, D_MODEL)),
                  _const_spec((1, D_MODEL))],
        out_specs=own(D_MODEL),
        out_shape=jax.ShapeDtypeStruct((BATCH, n_tiles * tile, D_MODEL), F32),
        compiler_params=pltpu.CompilerParams(
            dimension_semantics=("parallel", "parallel"), vmem_limit_bytes=VMEM_LIMIT,
            ),
        name="ffn_ctx" if seg.ctx else "ffn",
    )(h2, h2, h2, x1, mod, wu, cw, cb, wd, post)


def _rope_tables():
    rows = SEQ // GRID_W
    row = jnp.repeat(jnp.arange(rows), GRID_W).astype(F32)
    col = (jnp.arange(rows * GRID_W) % GRID_W).astype(F32)
    n_freq = HEAD_DIM // 4
    inv_freq = ROPE_THETA ** (-jnp.arange(n_freq, dtype=F32) / n_freq)
    ang_r = row[:, None] * inv_freq[None, :]
    ang_c = col[:, None] * inv_freq[None, :]
    ang = jnp.concatenate([ang_r, ang_r, ang_c, ang_c], axis=-1)
    cos = jnp.tile(jnp.cos(ang), (1, LANES // HEAD_DIM))
    sin = jnp.tile(jnp.sin(ang), (1, LANES // HEAD_DIM))
    upper = (jnp.arange(LANES) % 32) < 16
    sin_up = jnp.where(upper[None, :], -sin, 0.0)
    sin_dn = jnp.where(upper[None, :], 0.0, sin)
    pad = jnp.zeros((CTX_LEN, LANES), F32)
    return (jnp.concatenate([cos, jnp.ones((CTX_LEN, LANES), F32)], axis=0),
            jnp.concatenate([sin_up, pad], axis=0), jnp.concatenate([sin_dn, pad], axis=0))


def _permute_heads(w, axis):
    idx = jnp.concatenate([jnp.arange(h * HEAD_DIM, (h + 1) * HEAD_DIM) for h in HEAD_PERM])
    return jnp.take(w, idx, axis=axis)


def _layout_w_in(w):
    sizes = (GLA_QK, GLA_QK, GLA_WIDTH, GLA_WIDTH, 2 * GLA_GATE_RANK,
             Q_WIDTH, KV_WIDTH, KV_WIDTH, Q_WIDTH, KV_WIDTH, KV_WIDTH)
    parts, start = [], 0
    for size in sizes:
        parts.append(w[:, start:start + size])
        start += size
    a_q, a_k, a_v, a_r, a_z, b_q, b_k, b_v, c_q, c_k, c_v = parts
    z_pad = jnp.zeros((w.shape[0], LANES - 2 * GLA_GATE_RANK), w.dtype)
    return jnp.concatenate(
        [a_q, a_k, a_v, a_r, _permute_heads(b_q, 1), _permute_heads(c_q, 1), b_k, b_v, c_k, c_v,
         a_z, z_pad], axis=1).astype(BF16)


def _layout_w_out(w):
    a = w[:GLA_WIDTH]
    b = _permute_heads(w[GLA_WIDTH:GLA_WIDTH + Q_WIDTH], 0)
    c = _permute_heads(w[GLA_WIDTH + Q_WIDTH:], 0)
    return jnp.concatenate([a, b, c], axis=0).astype(BF16)


def _layout_gate(w_gate):
    out = jnp.zeros((2, LANES, GLA_QK), w_gate.dtype)
    for d in range(2):
        out = out.at[d, d * GLA_GATE_RANK:(d + 1) * GLA_GATE_RANK].set(w_gate[d])
    return out.astype(BF16)


def _diag_reduce_matrix():
    r = jnp.arange(GLA_SUB * LANES)
    c = jnp.arange(GLA_HEADS * GLA_SUB)
    same_s = (r[:, None] // LANES) == (c[None, :] % GLA_SUB)
    same_h = ((r[:, None] % LANES) // GLA_DK) == (c[None, :] // GLA_SUB)
    return jnp.where(same_s & same_h, 1.0, 0.0).astype(BF16)


def kernel(x, c, ctx, c_ctx, w_mod, b_mod, attn_pre_norm, attn_post_norm, ffn_pre_norm,
           ffn_post_norm, w_in, gla_w_gate, gla_b_gate, gla_out_norm, swa_sink, gqa_q_norm,
           gqa_k_norm, w_out, ffn_w_up, ffn_conv_w, ffn_conv_b, ffn_w_down):
    assert x.shape == (BATCH, SEQ, D_MODEL) and ctx.shape == (BATCH, CTX_LEN, D_MODEL)
    cos, sin_up, sin_dn = _rope_tables()
    wst = _diag_reduce_matrix()
    tri = _cumsum_matrices()
    cs = jnp.concatenate([c, c_ctx[None, :], jnp.zeros((16 - BATCH - 1, D_MODEL), F32)], axis=0)
    mod = _modulation(cs, w_mod, b_mod).reshape(DEPTH, 16, MOD_CHUNKS, D_MODEL)

    lat_proj, ctx_seg = _segment(False, PROJ_TILE), _segment(True, CTX_LEN)
    lat_ffn = _segment(False, FFN_TILE)
    x_lat, x_ctx = x, ctx
    for layer in range(DEPTH):
        with_ctx = layer < DEPTH - 1
        row = lambda v: v[layer][None, :]
        tile2 = lambda v: jnp.tile(v[layer], LANES // HEAD_DIM)[None, :]
        proj_args = (mod, layer, row(attn_pre_norm), _layout_w_in(w_in[layer]), cos, sin_up, sin_dn,
                     tile2(gqa_q_norm), tile2(gqa_k_norm))
        proj = _in_proj(x_lat, lat_proj, None, *proj_args)
        gq, gk, gv, gr, gz, bq, bk, bv, cq, ck, cv = _in_proj(x_ctx, ctx_seg, proj, *proj_args)
        o_f, o_b = _gla(gq, gk, gv, gz, _layout_gate(gla_w_gate[layer]),
                        gla_b_gate[layer][:, None, :], tri, wst)
        b_att = _swa(swa_sink[layer].reshape(KV_HEADS, GROUP), bq, bk, bv, with_ctx)
        c_att = _gqa(cq, ck, cv, with_ctx)
        out_args = (o_f, o_b, gr, b_att, c_att, mod, layer, _layout_w_out(w_out[layer]),
                    tile2(gla_out_norm), row(attn_post_norm), row(ffn_pre_norm))
        ffn_args = (mod, layer, ffn_w_up[layer].astype(BF16), ffn_conv_w[layer], row(ffn_conv_b),
                    ffn_w_down[layer].astype(BF16), row(ffn_post_norm))
        x1, h2 = _out_proj(x_lat, lat_proj, *out_args)
        x_lat = _ffn(h2, x1, lat_ffn, *ffn_args)
        if with_ctx:
            x1, h2 = _out_proj(x_ctx, ctx_seg, *out_args)
            x_ctx = _ffn(h2, x1, ctx_seg, *ffn_args)
    return x_lat
```

```python
import functools
import itertools
from typing import NamedTuple

import jax
import jax.numpy as jnp
from jax import lax
from jax.experimental import pallas as pl
from jax.experimental.pallas import tpu as pltpu

F32 = jnp.float32
BF16 = jnp.bfloat16

D_MODEL = 1024
BATCH = 8
SEQ = 2048
DEPTH = 2
CTX_LEN = 256
TOK = SEQ + CTX_LEN
GRID_W = 64
HEAD_DIM = 64
ROPE_THETA = 10000.0
NORM_EPS = 1e-6
MOD_CHUNKS = 6

GLA_HEADS = 4
GLA_DK = 32
GLA_DV = 64
GLA_GATE_RANK = 16
GLA_GATE_TAU = 16.0
GLA_QK = GLA_HEADS * GLA_DK
GLA_WIDTH = GLA_HEADS * GLA_DV

Q_HEADS = 6
KV_HEADS = 2
GROUP = Q_HEADS // KV_HEADS
Q_WIDTH = Q_HEADS * HEAD_DIM
KV_WIDTH = KV_HEADS * HEAD_DIM
SWA_BLOCK = 128

FFN_DIM = 2816
FFN_CHUNK = 256

LANES = 128
PROJ_TILE = 512
ROW_PART = 256
FFN_TILE = 1024
FFN_PART = 256
HALO = 16
GLA_BLOCK = 256
GLA_CHUNK = 64
GLA_SUB = 16
GLA_BATCH = 2
ATT_TQ = 512
ATT_ROWS = 256
VMEM_LIMIT = 56 * 1024 * 1024

C_GQ, C_GK, C_GV, C_GR = 0, 128, 256, 512
C_BQ, C_CQ = 768, 1152
C_BK, C_BV, C_CK, C_CV = 1536, 1664, 1792, 1920
C_GZ = 2048
IN_COLS = 2176
HEAD_PERM = (0, 3, 1, 4, 2, 5)

_NT = (((1,), (1,)), ((), ()))


class _Segment(NamedTuple):
    tile: int
    n_tiles: int
    first: int
    ctx: bool


def _segment(ctx, tile):
    if ctx:
        return _Segment(CTX_LEN, 1, SEQ // CTX_LEN, True)
    return _Segment(tile, SEQ // tile, 0, False)


def _dot(a, b):
    return jnp.dot(a, b, preferred_element_type=F32)


def _dot_nt(a, b):
    return lax.dot_general(a, b, _NT, preferred_element_type=F32)


def _idiv(x, n):
    assert n & (n - 1) == 0
    return lax.shift_right_logical(x, jnp.int32(n.bit_length() - 1))


def _split2(x):
    hi = x.astype(BF16)
    lo = (x - hi.astype(F32)).astype(BF16)
    return hi, lo


def _split3(x):
    h1 = x.astype(BF16)
    r1 = x - h1.astype(F32)
    h2 = r1.astype(BF16)
    h3 = (r1 - h2.astype(F32)).astype(BF16)
    return h1, h2, h3


def _seg_ones(width, seg):
    r = _idiv(lax.broadcasted_iota(jnp.int32, (width, width), 0), seg)
    c = _idiv(lax.broadcasted_iota(jnp.int32, (width, width), 1), seg)
    return jnp.where(r == c, 1.0, 0.0).astype(BF16)


def _seg_sum(x, ones):
    hi, lo = _split2(x)
    return _dot(hi, ones) + _dot(lo, ones)


def _row_rms(x, w):
    ms = jnp.mean(x * x, axis=-1, keepdims=True)
    return x * lax.rsqrt(ms + NORM_EPS) * w


def _head_rms(x, ones, w):
    ss = _seg_sum(x * x, ones)
    return x * lax.rsqrt(ss * (1.0 / HEAD_DIM) + NORM_EPS) * w


def _silu(x):
    return x / (1.0 + jnp.exp(-x))


def _rope(x, cos, sin_up, sin_dn):
    return x * cos + pltpu.roll(x, LANES - 16, 1) * sin_up + pltpu.roll(x, 16, 1) * sin_dn


def _const_spec(shape):
    return pl.BlockSpec(shape, lambda b, i: (0,) * len(shape))


def _params(*semantics):
    return pltpu.CompilerParams(dimension_semantics=semantics, vmem_limit_bytes=VMEM_LIMIT)


def _mod_kernel(c_ref, w_ref, b_ref, o_ref):
    a = _silu(c_ref[...]).astype(BF16)
    o_ref[0] = _dot(a, w_ref[0].astype(BF16)) + b_ref[0]


def _modulation(cs, w_mod, b_mod):
    tn = 1536
    n = MOD_CHUNKS * D_MODEL
    return pl.pallas_call(
        _mod_kernel,
        grid=(DEPTH, n // tn),
        in_specs=[
            pl.BlockSpec((16, D_MODEL), lambda l, j: (0, 0)),
            pl.BlockSpec((1, D_MODEL, tn), lambda l, j: (l, 0, j)),
            pl.BlockSpec((1, 1, tn), lambda l, j: (l, 0, j)),
        ],
        out_specs=pl.BlockSpec((1, 16, tn), lambda l, j: (l, 0, j)),
        out_shape=jax.ShapeDtypeStruct((DEPTH, 16, n), F32),
        compiler_params=_params("arbitrary", "arbitrary"),
        name="modulation",
    )(cs, w_mod, b_mod.reshape(DEPTH, 1, n))


def _mod_spec(layer, seg):
    return pl.BlockSpec((1, 1, MOD_CHUNKS, D_MODEL),
                        lambda b, i: (layer, BATCH if seg.ctx else b, 0, 0))


_PROJ_OUTPUTS = ((GLA_QK, F32), (GLA_QK, F32), (GLA_WIDTH, F32), (GLA_WIDTH, F32), (LANES, F32),
                 (Q_WIDTH, BF16), (KV_WIDTH, BF16), (KV_WIDTH, BF16),
                 (Q_WIDTH, BF16), (KV_WIDTH, BF16), (KV_WIDTH, BF16))


def _interleave(streams):
    for _ in itertools.zip_longest(*streams):
        pass


def _lag(stream, stages):
    for _ in range(stages):
        yield
    yield from stream


def _in_proj_kernel(x_ref, mod_ref, nw_ref, w_ref, cos_ref, su_ref, sd_ref, qn_ref, kn_ref, *rest):
    (gq_ref, gk_ref, gv_ref, gr_ref, gz_ref, bq_ref, bk_ref, bv_ref,
     cq_ref, ck_ref, cv_ref) = rest[-len(_PROJ_OUTPUTS):]
    sh = mod_ref[0, 0, 0:1, :]
    sc = mod_ref[0, 0, 1:2, :]
    ones = _seg_ones(LANES, HEAD_DIM)
    att_scale = HEAD_DIM ** -0.5

    def part(rows):
        h = (_row_rms(x_ref[0, rows, :], nw_ref[...]) * (1.0 + sc) + sh).astype(BF16)

        def proj(c0, width):
            return _dot(h, w_ref[:, c0:c0 + width])

        cos, su, sd = cos_ref[rows, :], su_ref[rows, :], sd_ref[rows, :]
        p_a = proj(C_GQ, C_BQ - C_GQ)
        p_q = proj(C_BQ, C_BK - C_BQ)
        yield
        gq_ref[0, rows, :] = p_a[:, C_GQ:C_GQ + GLA_QK] * (GLA_DK ** -0.5)
        gk_ref[0, rows, :] = p_a[:, C_GK:C_GK + GLA_QK]
        gv_ref[0, rows, :] = p_a[:, C_GV:C_GV + GLA_WIDTH]
        gr_ref[0, rows, :] = p_a[:, C_GR:C_GR + GLA_WIDTH]
        p_kv = proj(C_BK, IN_COLS - C_BK)

        def kv(c0):
            return p_kv[:, c0 - C_BK:c0 - C_BK + KV_WIDTH]
        yield
        for g in range(GROUP):
            ls = slice(g * LANES, (g + 1) * LANES)
            bq_ref[0, rows, ls] = (_rope(p_q[:, ls], cos, su, sd) * att_scale).astype(BF16)
            cq = _head_rms(p_q[:, Q_WIDTH + g * LANES:Q_WIDTH + (g + 1) * LANES], ones, qn_ref[...])
            cq_ref[0, rows, ls] = (_rope(cq, cos, su, sd) * att_scale).astype(BF16)
        yield
        bk_ref[0, rows, :] = _rope(kv(C_BK), cos, su, sd).astype(BF16)
        bv_ref[0, rows, :] = kv(C_BV).astype(BF16)
        ck_ref[0, rows, :] = _rope(_head_rms(kv(C_CK), ones, kn_ref[...]), cos, su, sd).astype(BF16)
        cv_ref[0, rows, :] = kv(C_CV).astype(BF16)
        gz_ref[0, rows, :] = kv(C_GZ)
        yield

    tile = x_ref.shape[1]
    _interleave([part(slice(r, r + ROW_PART)) for r in range(0, tile, ROW_PART)])


def _in_proj(x_seg, seg, into, mod, layer, nw, w, cos, su, sd, qn, kn):
    tile, first = seg.tile, seg.first

    def rows(width):
        return pl.BlockSpec((1, tile, width), lambda b, i: (b, first + i, 0))

    def table():
        return pl.BlockSpec((tile, LANES), lambda b, i: (first + i, 0))

    in_specs = [pl.BlockSpec((1, tile, D_MODEL), lambda b, i: (b, i, 0)), _mod_spec(layer, seg),
                _const_spec((1, D_MODEL)), _const_spec((D_MODEL, IN_COLS)),
                table(), table(), table(), _const_spec((1, LANES)), _const_spec((1, LANES))]
    args = [x_seg, mod, nw, w, cos, su, sd, qn, kn]
    aliases = {}
    if into is not None:
        aliases = {len(args) + k: k for k in range(len(into))}
        in_specs += [pl.BlockSpec(memory_space=pl.ANY)] * len(into)
        args += list(into)
    return pl.pallas_call(
        _in_proj_kernel,
        grid=(BATCH, seg.n_tiles),
        in_specs=in_specs,
        out_specs=[rows(wd) for wd, _ in _PROJ_OUTPUTS],
        out_shape=[jax.ShapeDtypeStruct((BATCH, TOK, wd), dt) for wd, dt in _PROJ_OUTPUTS],
        input_output_aliases=aliases,
        compiler_params=_params("parallel", "parallel"),
        name="in_proj_ctx" if seg.ctx else "in_proj",
    )(*args)


def _lane_group_mask(width, group, h):
    lane = lax.broadcasted_iota(jnp.int32, (1, width), 1)
    return _idiv(lane, group) == h


def _gla_direction(q_ref, k_ref, v_ref, z_ref, wg, bg, tri, wst_ref, o_ref, st_ref, b_s, e_s, rev):
    nb, c_len, sb = GLA_BLOCK, GLA_CHUNK, GLA_SUB
    n_chunks, n_sub = nb // c_len, c_len // sb

    g = _dot(z_ref[0].astype(BF16), wg) + bg
    la = (jnp.minimum(g, 0.0) - jnp.log1p(jnp.exp(-jnp.abs(g)))) * (1.0 / GLA_GATE_TAU)

    h1, h2, h3 = _split3(la)
    b = _dot(tri, h1) + _dot(tri, h2) + _dot(tri, h3)
    b_s[...] = b
    yield

    q, k, v = q_ref[0], k_ref[0], v_ref[0]

    half = sb // 2
    t_half = lax.broadcasted_iota(jnp.int32, (half, 1), 0)

    for jb in range(nb // sb):
        r0 = jb * sb
        bb = [b_s[pl.ds(r0 + lo, half), :] for lo in (0, half)]
        qb = [q_ref[0, pl.ds(r0 + lo, half), :] for lo in (0, half)]
        for s in range(sb):
            bs = b_s[pl.ds(r0 + s, 1), :]
            ks = k_ref[0, pl.ds(r0 + s, 1), :]
            halves = []
            for hi, lo in enumerate((0, half)):
                none = (lo > s) if rev else (lo + half - 1 < s)
                every = (lo + half - 1 <= s) if rev else (lo >= s)
                if none:
                    halves.append(jnp.zeros((half, LANES), F32))
                    continue
                diff = bb[hi] - bs
                if every:
                    e = qb[hi] * ks * jnp.exp(diff)
                else:
                    keep = (t_half + lo <= s) if rev else (t_half + lo >= s)
                    e = jnp.where(keep, qb[hi] * ks * jnp.exp(jnp.minimum(diff, 0.0)), 0.0)
                halves.append(e)
            e_s[pl.ds(r0, sb), s * LANES:(s + 1) * LANES] = jnp.concatenate(
                halves, axis=0).astype(BF16)
        if jb % 4 == 3:
            yield

    a_diag = _dot(e_s[...], wst_ref[...])
    yield

    hm_k = [_lane_group_mask(GLA_QK, GLA_DK, h) for h in range(GLA_HEADS)]
    hm_v = [_lane_group_mask(GLA_WIDTH, GLA_DV, h) for h in range(GLA_HEADS)]
    st_row = _idiv(lax.broadcasted_iota(jnp.int32, (GLA_WIDTH, GLA_QK), 0), GLA_DV)
    st_col = _idiv(lax.broadcasted_iota(jnp.int32, (GLA_WIDTH, GLA_QK), 1), GLA_DK)
    st_mask = st_row == st_col
    v_t = jnp.transpose(v).astype(BF16)
    chunk = [slice(c * c_len, (c + 1) * c_len) for c in range(n_chunks)]
    sub = [slice(j * sb, (j + 1) * sb) for j in range(n_sub)]

    q_hat, kv_t, d_row = [], [], []
    for c in range(n_chunks):
        bc = b[chunk[c]]
        end = 0 if rev else c_len - 1
        b_end = bc[end:end + 1]
        q_hat.append((q[chunk[c]] * jnp.exp(bc)).astype(BF16))
        d_row.append(jnp.exp(b_end))
        k_hat = (k[chunk[c]] * jnp.exp(b_end - bc)).astype(BF16)
        pads = [jnp.zeros((n, GLA_QK), BF16) for n in (c * c_len, nb - (c + 1) * c_len) if n]
        k_hat = jnp.concatenate(pads[:1 if c else 0] + [k_hat] + pads[1 if c else 0:], axis=0)
        kv_t.append(jnp.where(st_mask, _dot(v_t, k_hat), 0.0))
    yield

    scores = {}
    for c in range(n_chunks):
        bc, qc, kc = b[chunk[c]], q[chunk[c]], k[chunk[c]]
        for j in range(n_sub):
            if not ((j > 0) if rev else (j < n_sub - 1)):
                continue
            edge = j * sb if rev else (j + 1) * sb - 1
            ref_b = bc[edge:edge + 1]
            later = slice(0, j * sb) if rev else slice((j + 1) * sb, c_len)
            q_l = (qc[later] * jnp.exp(bc[later] - ref_b)).astype(BF16)
            k_j = kc[sub[j]] * jnp.exp(ref_b - bc[sub[j]])
            k_tile = jnp.concatenate(
                [jnp.where(hm_k[h], k_j, 0.0) for h in range(GLA_HEADS)], axis=0).astype(BF16)
            scores[(c, j)] = _dot_nt(q_l, k_tile)
        yield

    o_intra = []
    for c in range(n_chunks):
        vc = v[chunk[c]]
        blocks = [None] * n_sub
        for j in range(n_sub):
            v_exp = jnp.concatenate(
                [jnp.where(hm_v[h], vc[sub[j]], 0.0) for h in range(GLA_HEADS)],
                axis=0).astype(BF16)
            ad = a_diag[c * c_len + j * sb:c * c_len + (j + 1) * sb]
            if (c, j) in scores:
                sc = scores[(c, j)]
                p = jnp.concatenate([sc, ad], axis=0) if rev else jnp.concatenate([ad, sc], axis=0)
                first = 0 if rev else j
            else:
                p, first = ad, j
            contrib = _dot(p.astype(BF16), v_exp)
            for r in range(contrib.shape[0] // sb):
                piece = contrib[r * sb:(r + 1) * sb]
                blocks[first + r] = piece if blocks[first + r] is None else blocks[first + r] + piece
        o_intra.append(jnp.concatenate(blocks, axis=0))
        yield

    state = st_ref[...]
    outs = [None] * n_chunks
    for c in (reversed(range(n_chunks)) if rev else range(n_chunks)):
        outs[c] = o_intra[c] + _dot_nt(q_hat[c], state.astype(BF16))
        state = state * d_row[c] + kv_t[c]
        yield
    st_ref[...] = state
    o_ref[0, 0] = jnp.concatenate(outs, axis=0)


def _gla_kernel(qf, kf, vf, zf, qb, kb, vb, zb, wg_ref, bg_ref, tri_ref, wst_ref, of_ref, ob_ref,
                sf_ref, sb_ref, bf_s, bb_s, ef_s, eb_s):
    @pl.when(pl.program_id(1) == 0)
    def _():
        sf_ref[...] = jnp.zeros_like(sf_ref)
        sb_ref[...] = jnp.zeros_like(sb_ref)

    streams = []
    for bi in range(GLA_BATCH):
        one = pl.ds(bi, 1)
        streams.append(_gla_direction(
            qf.at[one], kf.at[one], vf.at[one], zf.at[one], wg_ref[0], bg_ref[0], tri_ref[0],
            wst_ref, of_ref.at[:, one], sf_ref.at[bi], bf_s.at[bi], ef_s.at[bi], False))
        streams.append(_gla_direction(
            qb.at[one], kb.at[one], vb.at[one], zb.at[one], wg_ref[1], bg_ref[1], tri_ref[1],
            wst_ref, ob_ref.at[:, one], sb_ref.at[bi], bb_s.at[bi], eb_s.at[bi], True))
    _interleave([_lag(s, n) for n, s in enumerate(streams)])


def _cumsum_matrices():
    t = jnp.arange(GLA_BLOCK)[:, None]
    s = jnp.arange(GLA_BLOCK)[None, :]
    inside = (t // GLA_CHUNK) == (s // GLA_CHUNK)
    return jnp.stack([jnp.where(inside & (s <= t), 1.0, 0.0),
                      jnp.where(inside & (s >= t), 1.0, 0.0)]).astype(BF16)


def _gla(gq, gk, gv, gz, wg, bg, tri, wst):
    n_lat = SEQ // GLA_BLOCK
    ctx_blk = n_lat

    def fwd(b, s):
        return (b, jnp.where(s == 0, ctx_blk, s - 1), 0)

    def bwd(b, s):
        return (b, jnp.where(s == 0, ctx_blk, n_lat - s), 0)

    def specs(index_map):
        return [pl.BlockSpec((GLA_BATCH, GLA_BLOCK, wd), index_map)
                for wd in (GLA_QK, GLA_QK, GLA_WIDTH, LANES)]

    return pl.pallas_call(
        _gla_kernel,
        grid=(BATCH // GLA_BATCH, n_lat + 1),
        in_specs=specs(fwd) + specs(bwd) + [
            _const_spec((2, LANES, GLA_QK)), _const_spec((2, 1, GLA_QK)),
            _const_spec((2, GLA_BLOCK, GLA_BLOCK)),
            _const_spec((GLA_SUB * LANES, GLA_HEADS * GLA_SUB))],
        out_specs=[pl.BlockSpec((1, GLA_BATCH, GLA_BLOCK, GLA_WIDTH), lambda b, s: (0,) + fwd(b, s)),
                   pl.BlockSpec((1, GLA_BATCH, GLA_BLOCK, GLA_WIDTH), lambda b, s: (0,) + bwd(b, s))],
        out_shape=[jax.ShapeDtypeStruct((1, BATCH, TOK, GLA_WIDTH), F32)] * 2,
        scratch_shapes=[pltpu.VMEM((GLA_BATCH, GLA_WIDTH, GLA_QK), F32)] * 2
        + [pltpu.VMEM((GLA_BATCH, GLA_BLOCK, GLA_QK), F32)] * 2
        + [pltpu.VMEM((GLA_BATCH, GLA_BLOCK, GLA_SUB * LANES), BF16)] * 2,
        compiler_params=_params("parallel", "arbitrary"),
        name="gla",
    )(gq, gk, gv, gz, gq, gk, gv, gz, wg, bg, tri, wst)


def _fold_lanes(x, op):
    acc = x[:, 0:LANES]
    for c in range(1, x.shape[1] // LANES):
        acc = op(acc, x[:, c * LANES:(c + 1) * LANES])
    return acc


class _Job(NamedTuple):
    rows: slice
    keys: jax.Array
    values: jax.Array
    bias: jax.Array


def _attend(q_ref, jobs, sink_ref, groups, ahead):
    lane = lax.broadcasted_iota(jnp.int32, (1, LANES), 1)
    first = lane < HEAD_DIM
    units = [(jb, h, gs) for jb in range(len(jobs)) for gs in groups for h in range(KV_HEADS)]

    def scores(jb, h, gs):
        mine = first if h == 0 else jnp.logical_not(first)
        qz = [jnp.where(mine, q_ref[0, jobs[jb].rows, g * LANES:(g + 1) * LANES], 0) for g in gs]
        qz = qz[0] if len(gs) == 1 else jnp.concatenate(qz, axis=0)
        return _dot_nt(qz, jobs[jb].keys)

    pending = {}
    for idx in range(min(ahead, len(units))):
        pending[idx] = scores(*units[idx])
    outs = {}
    for idx, (jb, h, gs) in enumerate(units):
        if idx + ahead < len(units):
            pending[idx + ahead] = scores(*units[idx + ahead])
        s = pending.pop(idx)
        job = jobs[jb]
        tq = job.rows.stop - job.rows.start
        if job.bias is not None:
            nb = job.bias.shape[1]
            s = jnp.concatenate([s[:, :nb] + job.bias[:s.shape[0]], s[:, nb:]], axis=1)
        m = jnp.max(_fold_lanes(s, jnp.maximum), axis=-1, keepdims=True)
        if sink_ref is not None:
            sk = [jnp.full((tq, 1), sink_ref[h, g], F32) for g in gs]
            sk = sk[0] if len(gs) == 1 else jnp.concatenate(sk, axis=0)
            m = jnp.maximum(m, sk)
        p = jnp.exp(s - m)
        den = jnp.sum(_fold_lanes(p, jnp.add), axis=-1, keepdims=True)
        if sink_ref is not None:
            den = den + jnp.exp(sk - m)
        o = _dot(p.astype(BF16), job.values) / den
        for r, g in enumerate(gs):
            outs[(jb, h, g)] = o[r * tq:(r + 1) * tq]
    return [jnp.concatenate([jnp.where(first, outs[(jb, 0, g)], outs[(jb, 1, g)])
                             for g in range(GROUP)], axis=1) for jb in range(len(jobs))]


_ALL_GROUPS = (tuple(range(GROUP)),)
_EACH_GROUP = tuple((g,) for g in range(GROUP))


def _swa_kernel(sink_ref, band_ref, q_ref, kc_ref, vc_ref, kp_ref, kk_ref, kn_ref, vp_ref, vk_ref,
                vn_ref, o_ref, *, n_lat):
    i = pl.program_id(1)
    w = SWA_BLOCK

    @pl.when(i >= n_lat)
    def _():
        job = _Job(slice(0, 2 * w), kc_ref[0], vc_ref[0], None)
        o_ref[0] = _attend(q_ref, [job], sink_ref, _ALL_GROUPS, 2)[0].astype(o_ref.dtype)

    @pl.when(i < n_lat)
    def _():
        k_lo, k_hi = kk_ref[0, 0:w, :], kk_ref[0, w:2 * w, :]
        v_lo, v_hi = vk_ref[0, 0:w, :], vk_ref[0, w:2 * w, :]
        col = lax.broadcasted_iota(jnp.int32, (1, 3 * w), 1)
        no_prev = jnp.where((col < w) & (i == 0), -1e30, 0.0)
        no_next = jnp.where((col >= 2 * w) & (i == n_lat - 1), -1e30, 0.0)
        jobs = [
            _Job(slice(0, w), jnp.concatenate([kp_ref[0], k_lo, k_hi, kc_ref[0]], axis=0),
                 jnp.concatenate([vp_ref[0], v_lo, v_hi, vc_ref[0]], axis=0),
                 band_ref[...] + no_prev),
            _Job(slice(w, 2 * w), jnp.concatenate([k_lo, k_hi, kn_ref[0], kc_ref[0]], axis=0),
                 jnp.concatenate([v_lo, v_hi, vn_ref[0], vc_ref[0]], axis=0),
                 band_ref[...] + no_next)]
        outs = _attend(q_ref, jobs, sink_ref, _ALL_GROUPS, 4)
        o_ref[0] = jnp.concatenate(outs, axis=0).astype(o_ref.dtype)


def _swa(sink, bq, bk, bv, with_ctx):
    w = SWA_BLOCK
    n_blocks = SEQ // w
    n_lat = n_blocks // 2
    n_steps = (TOK if with_ctx else SEQ) // (2 * w)

    def edge_block(which):
        return pl.BlockSpec(
            (1, w, KV_WIDTH),
            lambda b, i: (b, jnp.clip(2 * i - 1 if which < 0 else 2 * i + 2, 0, n_blocks - 1), 0))

    own_spec = pl.BlockSpec((1, 2 * w, KV_WIDTH), lambda b, i: (b, jnp.minimum(i, n_lat - 1), 0))

    t = jnp.arange(GROUP * w)[:, None] % w
    kk = jnp.arange(3 * w)[None, :]
    band = jnp.where((kk - t >= 0) & (kk - t <= 2 * w), 0.0, -1e30).astype(F32)

    ctx_spec = pl.BlockSpec((1, CTX_LEN, KV_WIDTH), lambda b, i: (b, SEQ // CTX_LEN, 0))
    return pl.pallas_call(
        functools.partial(_swa_kernel, n_lat=n_lat),
        grid=(BATCH, n_steps),
        in_specs=[pl.BlockSpec(memory_space=pltpu.SMEM),
                  _const_spec((GROUP * w, 3 * w)),
                  pl.BlockSpec((1, 2 * w, Q_WIDTH), lambda b, i: (b, i, 0)),
                  ctx_spec, ctx_spec,
                  edge_block(-1), own_spec, edge_block(1),
                  edge_block(-1), own_spec, edge_block(1)],
        out_specs=pl.BlockSpec((1, 2 * w, Q_WIDTH), lambda b, i: (b, i, 0)),
        out_shape=jax.ShapeDtypeStruct((BATCH, n_steps * 2 * w, Q_WIDTH), BF16),
        compiler_params=_params("parallel", "parallel"),
        name="swa",
    )(sink, band, bq, bk, bv, bk, bk, bk, bv, bv, bv)


def _gqa_kernel(q_ref, k_ref, v_ref, *rest, ctx):
    o_ref = rest[-1]
    if ctx:
        keys, values = k_ref[0, SEQ:TOK, :], v_ref[0, SEQ:TOK, :]
    else:
        keys, values = k_ref[0], v_ref[0]
    tq = q_ref.shape[1]
    jobs = [_Job(slice(r, r + ATT_ROWS), keys, values, None) for r in range(0, tq, ATT_ROWS)]
    outs = _attend(q_ref, jobs, None, _EACH_GROUP, 1)
    o_ref[0] = (outs[0] if len(outs) == 1 else jnp.concatenate(outs, axis=0)).astype(o_ref.dtype)


def _gqa(cq, ck, cv, seg, rows, into):
    tile, first = seg.tile, seg.first
    kv_spec = pl.BlockSpec((1, TOK, KV_WIDTH), lambda b, i: (b, 0, 0))
    q_spec = pl.BlockSpec((1, tile, Q_WIDTH), lambda b, i: (b, first + i, 0))
    in_specs, args, aliases = [q_spec, kv_spec, kv_spec], [cq, ck, cv], {}
    if into is not None:
        in_specs.append(pl.BlockSpec(memory_space=pl.ANY))
        args.append(into)
        aliases = {3: 0}
    return pl.pallas_call(
        functools.partial(_gqa_kernel, ctx=seg.ctx),
        grid=(BATCH, seg.n_tiles),
        in_specs=in_specs,
        out_specs=q_spec,
        out_shape=jax.ShapeDtypeStruct((BATCH, rows, Q_WIDTH), BF16),
        input_output_aliases=aliases,
        compiler_params=_params("parallel", "parallel"),
        name="gqa_ctx" if seg.ctx else "gqa",
    )(*args)


def _out_proj_kernel(x_ref, of_ref, ob_ref, r_ref, b_ref, c_ref, mod_ref, w_ref, on_ref,
                     post_ref, pre_ref, x1_ref, h2_ref):
    ones = _seg_ones(LANES, GLA_DV)
    g1 = mod_ref[0, 0, 2:3, :]
    sh2 = mod_ref[0, 0, 3:4, :]
    sc2 = mod_ref[0, 0, 4:5, :]

    def part(rows):
        mix = _dot(b_ref[0, rows, :], w_ref[GLA_WIDTH:GLA_WIDTH + Q_WIDTH, :])
        mix = mix + _dot(c_ref[0, rows, :], w_ref[GLA_WIDTH + Q_WIDTH:, :])
        ya = []
        for lp in range(GLA_WIDTH // LANES):
            ls = slice(lp * LANES, (lp + 1) * LANES)
            o = of_ref[0, 0, rows, ls] + ob_ref[0, 0, rows, ls]
            ya.append((_head_rms(o, ones, on_ref[...]) * _silu(r_ref[0, rows, ls])).astype(BF16))
        mix = mix + _dot(jnp.concatenate(ya, axis=1), w_ref[0:GLA_WIDTH, :])
        yield
        x1 = x_ref[0, rows, :] + g1 * _row_rms(mix, post_ref[...])
        x1_ref[0, rows, :] = x1
        h2_ref[0, rows, :] = (_row_rms(x1, pre_ref[...]) * (1.0 + sc2) + sh2).astype(BF16)
        yield

    tile = x_ref.shape[1]
    _interleave([part(slice(r, r + ROW_PART)) for r in range(0, tile, ROW_PART)])


def _out_proj(x_seg, seg, o_f, o_b, gr, b_att, c_att, mod, layer, w, on, post, pre):
    tile, first = seg.tile, seg.first

    def own(width):
        return pl.BlockSpec((1, tile, width), lambda b, i: (b, i, 0))

    def combined(width):
        return pl.BlockSpec((1, tile, width), lambda b, i: (b, first + i, 0))

    gla_spec = pl.BlockSpec((1, 1, tile, GLA_WIDTH), lambda b, i: (0, b, first + i, 0))
    rows = seg.n_tiles * tile
    return pl.pallas_call(
        _out_proj_kernel,
        grid=(BATCH, seg.n_tiles),
        in_specs=[own(D_MODEL), gla_spec, gla_spec, combined(GLA_WIDTH),
                  combined(Q_WIDTH), combined(Q_WIDTH), _mod_spec(layer, seg),
                  _const_spec((D_MODEL, D_MODEL)), _const_spec((1, LANES)),
                  _const_spec((1, D_MODEL)), _const_spec((1, D_MODEL))],
        out_specs=[own(D_MODEL), own(D_MODEL)],
        out_shape=[jax.ShapeDtypeStruct((BATCH, rows, D_MODEL), F32),
                   jax.ShapeDtypeStruct((BATCH, rows, D_MODEL), BF16)],
        compiler_params=_params("parallel", "parallel"),
        name="out_proj_ctx" if seg.ctx else "out_proj",
    )(x_seg, o_f, o_b, gr, b_att, c_att, mod, w, on, post, pre)


def _ffn_kernel(hp_ref, h_ref, hn_ref, x1_ref, mod_ref, wu_ref, cw_ref, cb_ref, wd_ref, post_ref,
                o_ref, *, n_tiles):
    i = pl.program_id(1)
    tile = h_ref.shape[1]
    hp = jnp.where(i != 0, hp_ref[0], 0)
    hn = jnp.where(i != n_tiles - 1, hn_ref[0], 0)
    h_ext = jnp.concatenate([hp, h_ref[0], hn], axis=0)
    part = min(tile, FFN_PART)
    win = part + 2 * HALO
    mid = slice(HALO, HALO + part)

    def conv(u, c0):
        cw = cw_ref[:, c0:c0 + FFN_CHUNK]
        prev = pltpu.roll(u, 1, 0)[mid]
        nxt = pltpu.roll(u, win - 1, 0)[mid]
        return cw[0:1] * prev + cw[1:2] * u[mid] + cw[2:3] * nxt + cb_ref[:, c0:c0 + FFN_CHUNK]

    n_fc = FFN_DIM // FFN_CHUNK

    n_parts = tile // part
    cuts = [0] + [HALO + r * part for r in range(1, n_parts)] + [tile + 2 * HALO]
    lhs = [h_ext[cuts[r]:cuts[r + 1]] for r in range(n_parts)]

    def window(us, r):
        lo, hi = r * part, r * part + win
        pieces = []
        for q in range(n_parts):
            a, b = max(lo, cuts[q]), min(hi, cuts[q + 1])
            if a < b:
                pieces.append(us[q][a - cuts[q]:b - cuts[q]])
        return pieces[0] if len(pieces) == 1 else jnp.concatenate(pieces, axis=0)

    def up(fc):
        ca = fc * FFN_CHUNK
        cg = FFN_DIM + ca
        return ([_dot(x, wu_ref[:, ca:ca + FFN_CHUNK]) for x in lhs],
                [_dot(x, wu_ref[:, cg:cg + FFN_CHUNK]) for x in lhs])

    acc = [jnp.zeros((part, D_MODEL), F32) for _ in range(n_parts)]
    ua, ug = up(0)
    for fc in range(n_fc):
        nxt_u = up(fc + 1) if fc + 1 < n_fc else None
        ca = fc * FFN_CHUNK
        for r in range(n_parts):
            act = (_silu(conv(window(ua, r), ca)) * conv(window(ug, r), FFN_DIM + ca)).astype(BF16)
            acc[r] = acc[r] + _dot(act, wd_ref[ca:ca + FFN_CHUNK, :])
        if nxt_u is not None:
            ua, ug = nxt_u
    g2 = mod_ref[0, 0, 5:6, :]
    for r in range(tile // part):
        rows = slice(r * part, (r + 1) * part)
        o_ref[0, rows, :] = x1_ref[0, rows, :] + g2 * _row_rms(acc[r], post_ref[...])


def _ffn(h2, x1, seg, mod, layer, wu, cw, cb, wd, post):
    tile, n_tiles = seg.tile, seg.n_tiles
    per = tile // HALO
    n_halo = n_tiles * per

    def own(width):
        return pl.BlockSpec((1, tile, width), lambda b, i: (b, i, 0))

    prev_spec = pl.BlockSpec((1, HALO, D_MODEL), lambda b, i: (b, jnp.maximum(i * per - 1, 0), 0))
    next_spec = pl.BlockSpec((1, HALO, D_MODEL),
                             lambda b, i: (b, jnp.minimum((i + 1) * per, n_halo - 1), 0))
    return pl.pallas_call(
        functools.partial(_ffn_kernel, n_tiles=n_tiles),
        grid=(BATCH, n_tiles),
        in_specs=[prev_spec, own(D_MODEL), next_spec, own(D_MODEL), _mod_spec(layer, seg),
                  _const_spec((D_MODEL, 2 * FFN_DIM)), _const_spec((3, 2 * FFN_DIM)),
                  _const_spec((1, 2 * FFN_DIM)), _const_spec((FFN_DIM, D_MODEL)),
                  _const_spec((1, D_MODEL))],
        out_specs=own(D_MODEL),
        out_shape=jax.ShapeDtypeStruct((BATCH, n_tiles * tile, D_MODEL), F32),
        compiler_params=pltpu.CompilerParams(
            dimension_semantics=("parallel", "parallel"), vmem_limit_bytes=VMEM_LIMIT,
            ),
        name="ffn_ctx" if seg.ctx else "ffn",
    )(h2, h2, h2, x1, mod, wu, cw, cb, wd, post)


def _rope_tables():
    rows = SEQ // GRID_W
    row = jnp.repeat(jnp.arange(rows), GRID_W).astype(F32)
    col = (jnp.arange(rows * GRID_W) % GRID_W).astype(F32)
    n_freq = HEAD_DIM // 4
    inv_freq = ROPE_THETA ** (-jnp.arange(n_freq, dtype=F32) / n_freq)
    ang_r = row[:, None] * inv_freq[None, :]
    ang_c = col[:, None] * inv_freq[None, :]
    ang = jnp.concatenate([ang_r, ang_r, ang_c, ang_c], axis=-1)
    cos = jnp.tile(jnp.cos(ang), (1, LANES // HEAD_DIM))
    sin = jnp.tile(jnp.sin(ang), (1, LANES // HEAD_DIM))
    upper = (jnp.arange(LANES) % 32) < 16
    sin_up = jnp.where(upper[None, :], -sin, 0.0)
    sin_dn = jnp.where(upper[None, :], 0.0, sin)
    pad = jnp.zeros((CTX_LEN, LANES), F32)
    return (jnp.concatenate([cos, jnp.ones((CTX_LEN, LANES), F32)], axis=0),
            jnp.concatenate([sin_up, pad], axis=0), jnp.concatenate([sin_dn, pad], axis=0))


def _permute_heads(w, axis):
    idx = jnp.concatenate([jnp.arange(h * HEAD_DIM, (h + 1) * HEAD_DIM) for h in HEAD_PERM])
    return jnp.take(w, idx, axis=axis)


def _layout_w_in(w):
    sizes = (GLA_QK, GLA_QK, GLA_WIDTH, GLA_WIDTH, 2 * GLA_GATE_RANK,
             Q_WIDTH, KV_WIDTH, KV_WIDTH, Q_WIDTH, KV_WIDTH, KV_WIDTH)
    parts, start = [], 0
    for size in sizes:
        parts.append(w[:, start:start + size])
        start += size
    a_q, a_k, a_v, a_r, a_z, b_q, b_k, b_v, c_q, c_k, c_v = parts
    z_pad = jnp.zeros((w.shape[0], LANES - 2 * GLA_GATE_RANK), w.dtype)
    return jnp.concatenate(
        [a_q, a_k, a_v, a_r, _permute_heads(b_q, 1), _permute_heads(c_q, 1), b_k, b_v, c_k, c_v,
         a_z, z_pad], axis=1).astype(BF16)


def _layout_w_out(w):
    a = w[:GLA_WIDTH]
    b = _permute_heads(w[GLA_WIDTH:GLA_WIDTH + Q_WIDTH], 0)
    c = _permute_heads(w[GLA_WIDTH + Q_WIDTH:], 0)
    return jnp.concatenate([a, b, c], axis=0).astype(BF16)


def _layout_gate(w_gate):
    out = jnp.zeros((2, LANES, GLA_QK), w_gate.dtype)
    for d in range(2):
        out = out.at[d, d * GLA_GATE_RANK:(d + 1) * GLA_GATE_RANK].set(w_gate[d])
    return out.astype(BF16)


def _diag_reduce_matrix():
    r = jnp.arange(GLA_SUB * LANES)
    c = jnp.arange(GLA_HEADS * GLA_SUB)
    same_s = (r[:, None] // LANES) == (c[None, :] % GLA_SUB)
    same_h = ((r[:, None] % LANES) // GLA_DK) == (c[None, :] // GLA_SUB)
    return jnp.where(same_s & same_h, 1.0, 0.0).astype(BF16)


def kernel(x, c, ctx, c_ctx, w_mod, b_mod, attn_pre_norm, attn_post_norm, ffn_pre_norm,
           ffn_post_norm, w_in, gla_w_gate, gla_b_gate, gla_out_norm, swa_sink, gqa_q_norm,
           gqa_k_norm, w_out, ffn_w_up, ffn_conv_w, ffn_conv_b, ffn_w_down):
    assert x.shape == (BATCH, SEQ, D_MODEL) and ctx.shape == (BATCH, CTX_LEN, D_MODEL)
    cos, sin_up, sin_dn = _rope_tables()
    wst = _diag_reduce_matrix()
    tri = _cumsum_matrices()
    cs = jnp.concatenate([c, c_ctx[None, :], jnp.zeros((16 - BATCH - 1, D_MODEL), F32)], axis=0)
    mod = _modulation(cs, w_mod, b_mod).reshape(DEPTH, 16, MOD_CHUNKS, D_MODEL)

    lat_proj, ctx_seg = _segment(False, PROJ_TILE), _segment(True, CTX_LEN)
    lat_ffn = _segment(False, FFN_TILE)
    lat_att = _segment(False, ATT_TQ)
    x_lat, x_ctx = x, ctx
    for layer in range(DEPTH):
        with_ctx = layer < DEPTH - 1
        row = lambda v: v[layer][None, :]
        tile2 = lambda v: jnp.tile(v[layer], LANES // HEAD_DIM)[None, :]
        proj_args = (mod, layer, row(attn_pre_norm), _layout_w_in(w_in[layer]), cos, sin_up, sin_dn,
                     tile2(gqa_q_norm), tile2(gqa_k_norm))
        proj = _in_proj(x_lat, lat_proj, None, *proj_args)
        gq, gk, gv, gr, gz, bq, bk, bv, cq, ck, cv = _in_proj(x_ctx, ctx_seg, proj, *proj_args)
        o_f, o_b = _gla(gq, gk, gv, gz, _layout_gate(gla_w_gate[layer]),
                        gla_b_gate[layer][:, None, :], tri, wst)
        b_att = _swa(swa_sink[layer].reshape(KV_HEADS, GROUP), bq, bk, bv, with_ctx)
        c_att = _gqa(cq, ck, cv, lat_att, TOK if with_ctx else SEQ, None)
        if with_ctx:
            c_att = _gqa(cq, ck, cv, ctx_seg, TOK, c_att)
        out_args = (o_f, o_b, gr, b_att, c_att, mod, layer, _layout_w_out(w_out[layer]),
                    tile2(gla_out_norm), row(attn_post_norm), row(ffn_pre_norm))
        ffn_args = (mod, layer, ffn_w_up[layer].astype(BF16), ffn_conv_w[layer], row(ffn_conv_b),
                    ffn_w_down[layer].astype(BF16), row(ffn_post_norm))
        x1, h2 = _out_proj(x_lat, lat_proj, *out_args)
        x_lat = _ffn(h2, x1, lat_ffn, *ffn_args)
        if with_ctx:
            x1, h2 = _out_proj(x_ctx, ctx_seg, *out_args)
            x_ctx = _ffn(h2, x1, ctx_seg, *ffn_args)
    return x_lat
```

```python
import functools
import itertools
from typing import NamedTuple

import jax
import jax.numpy as jnp
from jax import lax
from jax.experimental import pallas as pl
from jax.experimental.pallas import tpu as pltpu

F32 = jnp.float32
BF16 = jnp.bfloat16

D_MODEL = 1024
BATCH = 8
SEQ = 2048
DEPTH = 2
CTX_LEN = 256
TOK = SEQ + CTX_LEN
GRID_W = 64
HEAD_DIM = 64
ROPE_THETA = 10000.0
NORM_EPS = 1e-6
MOD_CHUNKS = 6

GLA_HEADS = 4
GLA_DK = 32
GLA_DV = 64
GLA_GATE_RANK = 16
GLA_GATE_TAU = 16.0
GLA_QK = GLA_HEADS * GLA_DK
GLA_WIDTH = GLA_HEADS * GLA_DV

Q_HEADS = 6
KV_HEADS = 2
GROUP = Q_HEADS // KV_HEADS
Q_WIDTH = Q_HEADS * HEAD_DIM
KV_WIDTH = KV_HEADS * HEAD_DIM
SWA_BLOCK = 128

FFN_DIM = 2816
FFN_CHUNK = 256

LANES = 128
PROJ_TILE = 1024
ROW_PART = 256
FFN_TILE = 1024
FFN_PART = 256
HALO = 16
GLA_BLOCK = 256
GLA_CHUNK = 64
GLA_SUB = 16
GLA_BATCH = 2
ATT_TQ = 1024
ATT_ROWS = 256
VMEM_LIMIT = 56 * 1024 * 1024

C_GQ, C_GK, C_GV, C_GR = 0, 128, 256, 512
C_BQ, C_CQ = 768, 1152
C_BK, C_BV, C_CK, C_CV = 1536, 1664, 1792, 1920
C_GZ = 2048
IN_COLS = 2176
HEAD_PERM = (0, 3, 1, 4, 2, 5)

_NT = (((1,), (1,)), ((), ()))


class _Segment(NamedTuple):
    tile: int
    n_tiles: int
    first: int
    ctx: bool


def _segment(ctx, tile):
    if ctx:
        return _Segment(CTX_LEN, 1, SEQ // CTX_LEN, True)
    return _Segment(tile, SEQ // tile, 0, False)


def _dot(a, b):
    return jnp.dot(a, b, preferred_element_type=F32)


def _dot_nt(a, b):
    return lax.dot_general(a, b, _NT, preferred_element_type=F32)


def _idiv(x, n):
    assert n & (n - 1) == 0
    return lax.shift_right_logical(x, jnp.int32(n.bit_length() - 1))


def _split2(x):
    hi = x.astype(BF16)
    lo = (x - hi.astype(F32)).astype(BF16)
    return hi, lo


def _split3(x):
    h1 = x.astype(BF16)
    r1 = x - h1.astype(F32)
    h2 = r1.astype(BF16)
    h3 = (r1 - h2.astype(F32)).astype(BF16)
    return h1, h2, h3


def _seg_ones(width, seg):
    r = _idiv(lax.broadcasted_iota(jnp.int32, (width, width), 0), seg)
    c = _idiv(lax.broadcasted_iota(jnp.int32, (width, width), 1), seg)
    return jnp.where(r == c, 1.0, 0.0).astype(BF16)


def _seg_sum(x, ones):
    hi, lo = _split2(x)
    return _dot(hi, ones) + _dot(lo, ones)


def _row_rms(x, w):
    ms = jnp.mean(x * x, axis=-1, keepdims=True)
    return x * lax.rsqrt(ms + NORM_EPS) * w


def _head_rms(x, ones, w):
    ss = _seg_sum(x * x, ones)
    return x * lax.rsqrt(ss * (1.0 / HEAD_DIM) + NORM_EPS) * w


def _silu(x):
    return x / (1.0 + jnp.exp(-x))


def _rope(x, cos, sin_up, sin_dn):
    return x * cos + pltpu.roll(x, LANES - 16, 1) * sin_up + pltpu.roll(x, 16, 1) * sin_dn


def _const_spec(shape):
    return pl.BlockSpec(shape, lambda b, i: (0,) * len(shape))


def _params(*semantics):
    return pltpu.CompilerParams(dimension_semantics=semantics, vmem_limit_bytes=VMEM_LIMIT)


def _mod_kernel(c_ref, w_ref, b_ref, o_ref):
    a = _silu(c_ref[...]).astype(BF16)
    o_ref[0] = _dot(a, w_ref[0].astype(BF16)) + b_ref[0]


def _modulation(cs, w_mod, b_mod):
    tn = 1536
    n = MOD_CHUNKS * D_MODEL
    return pl.pallas_call(
        _mod_kernel,
        grid=(DEPTH, n // tn),
        in_specs=[
            pl.BlockSpec((16, D_MODEL), lambda l, j: (0, 0)),
            pl.BlockSpec((1, D_MODEL, tn), lambda l, j: (l, 0, j)),
            pl.BlockSpec((1, 1, tn), lambda l, j: (l, 0, j)),
        ],
        out_specs=pl.BlockSpec((1, 16, tn), lambda l, j: (l, 0, j)),
        out_shape=jax.ShapeDtypeStruct((DEPTH, 16, n), F32),
        compiler_params=_params("arbitrary", "arbitrary"),
        name="modulation",
    )(cs, w_mod, b_mod.reshape(DEPTH, 1, n))


def _mod_spec(layer, seg):
    return pl.BlockSpec((1, 1, MOD_CHUNKS, D_MODEL),
                        lambda b, i: (layer, BATCH if seg.ctx else b, 0, 0))


_PROJ_OUTPUTS = ((GLA_QK, F32), (GLA_QK, F32), (GLA_WIDTH, F32), (GLA_WIDTH, F32), (LANES, F32),
                 (Q_WIDTH, BF16), (KV_WIDTH, BF16), (KV_WIDTH, BF16),
                 (Q_WIDTH, BF16), (KV_WIDTH, BF16), (KV_WIDTH, BF16))


def _interleave(streams):
    for _ in itertools.zip_longest(*streams):
        pass


def _lag(stream, stages):
    for _ in range(stages):
        yield
    yield from stream


def _in_proj_kernel(x_ref, mod_ref, nw_ref, w_ref, cos_ref, su_ref, sd_ref, qn_ref, kn_ref, *rest):
    (gq_ref, gk_ref, gv_ref, gr_ref, gz_ref, bq_ref, bk_ref, bv_ref,
     cq_ref, ck_ref, cv_ref) = rest[-len(_PROJ_OUTPUTS):]
    sh = mod_ref[0, 0, 0:1, :]
    sc = mod_ref[0, 0, 1:2, :]
    ones = _seg_ones(LANES, HEAD_DIM)
    att_scale = HEAD_DIM ** -0.5

    def part(rows):
        h = (_row_rms(x_ref[0, rows, :], nw_ref[...]) * (1.0 + sc) + sh).astype(BF16)

        def proj(c0, width):
            return _dot(h, w_ref[:, c0:c0 + width])

        cos, su, sd = cos_ref[rows, :], su_ref[rows, :], sd_ref[rows, :]
        p_a = proj(C_GQ, C_BQ - C_GQ)
        p_q = proj(C_BQ, C_BK - C_BQ)
        yield
        gq_ref[0, rows, :] = p_a[:, C_GQ:C_GQ + GLA_QK] * (GLA_DK ** -0.5)
        gk_ref[0, rows, :] = p_a[:, C_GK:C_GK + GLA_QK]
        gv_ref[0, rows, :] = p_a[:, C_GV:C_GV + GLA_WIDTH]
        gr_ref[0, rows, :] = p_a[:, C_GR:C_GR + GLA_WIDTH]
        p_kv = proj(C_BK, IN_COLS - C_BK)

        def kv(c0):
            return p_kv[:, c0 - C_BK:c0 - C_BK + KV_WIDTH]
        yield
        for g in range(GROUP):
            ls = slice(g * LANES, (g + 1) * LANES)
            bq_ref[0, rows, ls] = (_rope(p_q[:, ls], cos, su, sd) * att_scale).astype(BF16)
            cq = _head_rms(p_q[:, Q_WIDTH + g * LANES:Q_WIDTH + (g + 1) * LANES], ones, qn_ref[...])
            cq_ref[0, rows, ls] = (_rope(cq, cos, su, sd) * att_scale).astype(BF16)
        yield
        bk_ref[0, rows, :] = _rope(kv(C_BK), cos, su, sd).astype(BF16)
        bv_ref[0, rows, :] = kv(C_BV).astype(BF16)
        ck_ref[0, rows, :] = _rope(_head_rms(kv(C_CK), ones, kn_ref[...]), cos, su, sd).astype(BF16)
        cv_ref[0, rows, :] = kv(C_CV).astype(BF16)
        gz_ref[0, rows, :] = kv(C_GZ)
        yield

    tile = x_ref.shape[1]
    _interleave([part(slice(r, r + ROW_PART)) for r in range(0, tile, ROW_PART)])


def _in_proj(x_seg, seg, into, mod, layer, nw, w, cos, su, sd, qn, kn):
    tile, first = seg.tile, seg.first

    def rows(width):
        return pl.BlockSpec((1, tile, width), lambda b, i: (b, first + i, 0))

    def table():
        return pl.BlockSpec((tile, LANES), lambda b, i: (first + i, 0))

    in_specs = [pl.BlockSpec((1, tile, D_MODEL), lambda b, i: (b, i, 0)), _mod_spec(layer, seg),
                _const_spec((1, D_MODEL)), _const_spec((D_MODEL, IN_COLS)),
                table(), table(), table(), _const_spec((1, LANES)), _const_spec((1, LANES))]
    args = [x_seg, mod, nw, w, cos, su, sd, qn, kn]
    aliases = {}
    if into is not None:
        aliases = {len(args) + k: k for k in range(len(into))}
        in_specs += [pl.BlockSpec(memory_space=pl.ANY)] * len(into)
        args += list(into)
    return pl.pallas_call(
        _in_proj_kernel,
        grid=(BATCH, seg.n_tiles),
        in_specs=in_specs,
        out_specs=[rows(wd) for wd, _ in _PROJ_OUTPUTS],
        out_shape=[jax.ShapeDtypeStruct((BATCH, TOK, wd), dt) for wd, dt in _PROJ_OUTPUTS],
        input_output_aliases=aliases,
        compiler_params=_params("parallel", "parallel"),
        name="in_proj_ctx" if seg.ctx else "in_proj",
    )(*args)


def _lane_group_mask(width, group, h):
    lane = lax.broadcasted_iota(jnp.int32, (1, width), 1)
    return _idiv(lane, group) == h


def _gla_direction(q_ref, k_ref, v_ref, z_ref, wg, bg, tri, wst_ref, o_ref, st_ref, b_s, e_s, rev):
    nb, c_len, sb = GLA_BLOCK, GLA_CHUNK, GLA_SUB
    n_chunks, n_sub = nb // c_len, c_len // sb

    g = _dot(z_ref[0].astype(BF16), wg) + bg
    la = (jnp.minimum(g, 0.0) - jnp.log1p(jnp.exp(-jnp.abs(g)))) * (1.0 / GLA_GATE_TAU)

    h1, h2, h3 = _split3(la)
    b = _dot(tri, h1) + _dot(tri, h2) + _dot(tri, h3)
    b_s[...] = b
    yield

    q, k, v = q_ref[0], k_ref[0], v_ref[0]

    half = sb // 2
    t_half = lax.broadcasted_iota(jnp.int32, (half, 1), 0)

    for jb in range(nb // sb):
        r0 = jb * sb
        bb = [b_s[pl.ds(r0 + lo, half), :] for lo in (0, half)]
        qb = [q_ref[0, pl.ds(r0 + lo, half), :] for lo in (0, half)]
        for s in range(sb):
            bs = b_s[pl.ds(r0 + s, 1), :]
            ks = k_ref[0, pl.ds(r0 + s, 1), :]
            halves = []
            for hi, lo in enumerate((0, half)):
                none = (lo > s) if rev else (lo + half - 1 < s)
                every = (lo + half - 1 <= s) if rev else (lo >= s)
                if none:
                    halves.append(jnp.zeros((half, LANES), F32))
                    continue
                diff = bb[hi] - bs
                if every:
                    e = qb[hi] * ks * jnp.exp(diff)
                else:
                    keep = (t_half + lo <= s) if rev else (t_half + lo >= s)
                    e = jnp.where(keep, qb[hi] * ks * jnp.exp(jnp.minimum(diff, 0.0)), 0.0)
                halves.append(e)
            e_s[pl.ds(r0, sb), s * LANES:(s + 1) * LANES] = jnp.concatenate(
                halves, axis=0).astype(BF16)
        if jb % 4 == 3:
            yield

    a_diag = _dot(e_s[...], wst_ref[...])
    yield

    hm_k = [_lane_group_mask(GLA_QK, GLA_DK, h) for h in range(GLA_HEADS)]
    hm_v = [_lane_group_mask(GLA_WIDTH, GLA_DV, h) for h in range(GLA_HEADS)]
    st_row = _idiv(lax.broadcasted_iota(jnp.int32, (GLA_WIDTH, GLA_QK), 0), GLA_DV)
    st_col = _idiv(lax.broadcasted_iota(jnp.int32, (GLA_WIDTH, GLA_QK), 1), GLA_DK)
    st_mask = st_row == st_col
    v_t = jnp.transpose(v).astype(BF16)
    chunk = [slice(c * c_len, (c + 1) * c_len) for c in range(n_chunks)]
    sub = [slice(j * sb, (j + 1) * sb) for j in range(n_sub)]

    q_hat, kv_t, d_row = [], [], []
    for c in range(n_chunks):
        bc = b[chunk[c]]
        end = 0 if rev else c_len - 1
        b_end = bc[end:end + 1]
        q_hat.append((q[chunk[c]] * jnp.exp(bc)).astype(BF16))
        d_row.append(jnp.exp(b_end))
        k_hat = (k[chunk[c]] * jnp.exp(b_end - bc)).astype(BF16)
        pads = [jnp.zeros((n, GLA_QK), BF16) for n in (c * c_len, nb - (c + 1) * c_len) if n]
        k_hat = jnp.concatenate(pads[:1 if c else 0] + [k_hat] + pads[1 if c else 0:], axis=0)
        kv_t.append(jnp.where(st_mask, _dot(v_t, k_hat), 0.0))
    yield

    scores = {}
    for c in range(n_chunks):
        bc, qc, kc = b[chunk[c]], q[chunk[c]], k[chunk[c]]
        for j in range(n_sub):
            if not ((j > 0) if rev else (j < n_sub - 1)):
                continue
            edge = j * sb if rev else (j + 1) * sb - 1
            ref_b = bc[edge:edge + 1]
            later = slice(0, j * sb) if rev else slice((j + 1) * sb, c_len)
            q_l = (qc[later] * jnp.exp(bc[later] - ref_b)).astype(BF16)
            k_j = kc[sub[j]] * jnp.exp(ref_b - bc[sub[j]])
            k_tile = jnp.concatenate(
                [jnp.where(hm_k[h], k_j, 0.0) for h in range(GLA_HEADS)], axis=0).astype(BF16)
            scores[(c, j)] = _dot_nt(q_l, k_tile)
        yield

    o_intra = []
    for c in range(n_chunks):
        vc = v[chunk[c]]
        blocks = [None] * n_sub
        for j in range(n_sub):
            v_exp = jnp.concatenate(
                [jnp.where(hm_v[h], vc[sub[j]], 0.0) for h in range(GLA_HEADS)],
                axis=0).astype(BF16)
            ad = a_diag[c * c_len + j * sb:c * c_len + (j + 1) * sb]
            if (c, j) in scores:
                sc = scores[(c, j)]
                p = jnp.concatenate([sc, ad], axis=0) if rev else jnp.concatenate([ad, sc], axis=0)
                first = 0 if rev else j
            else:
                p, first = ad, j
            contrib = _dot(p.astype(BF16), v_exp)
            for r in range(contrib.shape[0] // sb):
                piece = contrib[r * sb:(r + 1) * sb]
                blocks[first + r] = piece if blocks[first + r] is None else blocks[first + r] + piece
        o_intra.append(jnp.concatenate(blocks, axis=0))
        yield

    state = st_ref[...]
    outs = [None] * n_chunks
    for c in (reversed(range(n_chunks)) if rev else range(n_chunks)):
        outs[c] = o_intra[c] + _dot_nt(q_hat[c], state.astype(BF16))
        state = state * d_row[c] + kv_t[c]
        yield
    st_ref[...] = state
    o_ref[0, 0] = jnp.concatenate(outs, axis=0)


def _gla_kernel(qf, kf, vf, zf, qb, kb, vb, zb, wg_ref, bg_ref, tri_ref, wst_ref, of_ref, ob_ref,
                sf_ref, sb_ref, bf_s, bb_s, ef_s, eb_s):
    @pl.when(pl.program_id(1) == 0)
    def _():
        sf_ref[...] = jnp.zeros_like(sf_ref)
        sb_ref[...] = jnp.zeros_like(sb_ref)

    streams = []
    for bi in range(GLA_BATCH):
        one = pl.ds(bi, 1)
        streams.append(_gla_direction(
            qf.at[one], kf.at[one], vf.at[one], zf.at[one], wg_ref[0], bg_ref[0], tri_ref[0],
            wst_ref, of_ref.at[:, one], sf_ref.at[bi], bf_s.at[bi], ef_s.at[bi], False))
        streams.append(_gla_direction(
            qb.at[one], kb.at[one], vb.at[one], zb.at[one], wg_ref[1], bg_ref[1], tri_ref[1],
            wst_ref, ob_ref.at[:, one], sb_ref.at[bi], bb_s.at[bi], eb_s.at[bi], True))
    _interleave([_lag(s, n) for n, s in enumerate(streams)])


def _cumsum_matrices():
    t = jnp.arange(GLA_BLOCK)[:, None]
    s = jnp.arange(GLA_BLOCK)[None, :]
    inside = (t // GLA_CHUNK) == (s // GLA_CHUNK)
    return jnp.stack([jnp.where(inside & (s <= t), 1.0, 0.0),
                      jnp.where(inside & (s >= t), 1.0, 0.0)]).astype(BF16)


def _gla(gq, gk, gv, gz, wg, bg, tri, wst):
    n_lat = SEQ // GLA_BLOCK
    ctx_blk = n_lat

    def fwd(b, s):
        return (b, jnp.where(s == 0, ctx_blk, s - 1), 0)

    def bwd(b, s):
        return (b, jnp.where(s == 0, ctx_blk, n_lat - s), 0)

    def specs(index_map):
        return [pl.BlockSpec((GLA_BATCH, GLA_BLOCK, wd), index_map)
                for wd in (GLA_QK, GLA_QK, GLA_WIDTH, LANES)]

    return pl.pallas_call(
        _gla_kernel,
        grid=(BATCH // GLA_BATCH, n_lat + 1),
        in_specs=specs(fwd) + specs(bwd) + [
            _const_spec((2, LANES, GLA_QK)), _const_spec((2, 1, GLA_QK)),
            _const_spec((2, GLA_BLOCK, GLA_BLOCK)),
            _const_spec((GLA_SUB * LANES, GLA_HEADS * GLA_SUB))],
        out_specs=[pl.BlockSpec((1, GLA_BATCH, GLA_BLOCK, GLA_WIDTH), lambda b, s: (0,) + fwd(b, s)),
                   pl.BlockSpec((1, GLA_BATCH, GLA_BLOCK, GLA_WIDTH), lambda b, s: (0,) + bwd(b, s))],
        out_shape=[jax.ShapeDtypeStruct((1, BATCH, TOK, GLA_WIDTH), F32)] * 2,
        scratch_shapes=[pltpu.VMEM((GLA_BATCH, GLA_WIDTH, GLA_QK), F32)] * 2
        + [pltpu.VMEM((GLA_BATCH, GLA_BLOCK, GLA_QK), F32)] * 2
        + [pltpu.VMEM((GLA_BATCH, GLA_BLOCK, GLA_SUB * LANES), BF16)] * 2,
        compiler_params=_params("parallel", "arbitrary"),
        name="gla",
    )(gq, gk, gv, gz, gq, gk, gv, gz, wg, bg, tri, wst)


def _fold_lanes(x, op):
    acc = x[:, 0:LANES]
    for c in range(1, x.shape[1] // LANES):
        acc = op(acc, x[:, c * LANES:(c + 1) * LANES])
    return acc


class _Job(NamedTuple):
    rows: slice
    keys: jax.Array
    values: jax.Array
    bias: jax.Array


def _attend(q_ref, jobs, sink_ref, groups, ahead):
    lane = lax.broadcasted_iota(jnp.int32, (1, LANES), 1)
    first = lane < HEAD_DIM
    units = [(jb, h, gs) for jb in range(len(jobs)) for gs in groups for h in range(KV_HEADS)]

    def scores(jb, h, gs):
        mine = first if h == 0 else jnp.logical_not(first)
        qz = [jnp.where(mine, q_ref[0, jobs[jb].rows, g * LANES:(g + 1) * LANES], 0) for g in gs]
        qz = qz[0] if len(gs) == 1 else jnp.concatenate(qz, axis=0)
        return _dot_nt(qz, jobs[jb].keys)

    pending = {}
    for idx in range(min(ahead, len(units))):
        pending[idx] = scores(*units[idx])
    outs = {}
    for idx, (jb, h, gs) in enumerate(units):
        if idx + ahead < len(units):
            pending[idx + ahead] = scores(*units[idx + ahead])
        s = pending.pop(idx)
        job = jobs[jb]
        tq = job.rows.stop - job.rows.start
        if job.bias is not None:
            nb = job.bias.shape[1]
            s = jnp.concatenate([s[:, :nb] + job.bias[:s.shape[0]], s[:, nb:]], axis=1)
        m = jnp.max(_fold_lanes(s, jnp.maximum), axis=-1, keepdims=True)
        if sink_ref is not None:
            sk = [jnp.full((tq, 1), sink_ref[h, g], F32) for g in gs]
            sk = sk[0] if len(gs) == 1 else jnp.concatenate(sk, axis=0)
            m = jnp.maximum(m, sk)
        p = jnp.exp(s - m)
        den = jnp.sum(_fold_lanes(p, jnp.add), axis=-1, keepdims=True)
        if sink_ref is not None:
            den = den + jnp.exp(sk - m)
        o = _dot(p.astype(BF16), job.values) / den
        for r, g in enumerate(gs):
            outs[(jb, h, g)] = o[r * tq:(r + 1) * tq]
    return [jnp.concatenate([jnp.where(first, outs[(jb, 0, g)], outs[(jb, 1, g)])
                             for g in range(GROUP)], axis=1) for jb in range(len(jobs))]


_ALL_GROUPS = (tuple(range(GROUP)),)
_EACH_GROUP = tuple((g,) for g in range(GROUP))


def _swa_kernel(sink_ref, band_ref, q_ref, kc_ref, vc_ref, kp_ref, kk_ref, kn_ref, vp_ref, vk_ref,
                vn_ref, o_ref, *, n_lat):
    i = pl.program_id(1)
    w = SWA_BLOCK

    @pl.when(i >= n_lat)
    def _():
        job = _Job(slice(0, 2 * w), kc_ref[0], vc_ref[0], None)
        o_ref[0] = _attend(q_ref, [job], sink_ref, _ALL_GROUPS, 2)[0].astype(o_ref.dtype)

    @pl.when(i < n_lat)
    def _():
        k_lo, k_hi = kk_ref[0, 0:w, :], kk_ref[0, w:2 * w, :]
        v_lo, v_hi = vk_ref[0, 0:w, :], vk_ref[0, w:2 * w, :]
        col = lax.broadcasted_iota(jnp.int32, (1, 3 * w), 1)
        no_prev = jnp.where((col < w) & (i == 0), -1e30, 0.0)
        no_next = jnp.where((col >= 2 * w) & (i == n_lat - 1), -1e30, 0.0)
        jobs = [
            _Job(slice(0, w), jnp.concatenate([kp_ref[0], k_lo, k_hi, kc_ref[0]], axis=0),
                 jnp.concatenate([vp_ref[0], v_lo, v_hi, vc_ref[0]], axis=0),
                 band_ref[...] + no_prev),
            _Job(slice(w, 2 * w), jnp.concatenate([k_lo, k_hi, kn_ref[0], kc_ref[0]], axis=0),
                 jnp.concatenate([v_lo, v_hi, vn_ref[0], vc_ref[0]], axis=0),
                 band_ref[...] + no_next)]
        outs = _attend(q_ref, jobs, sink_ref, _ALL_GROUPS, 4)
        o_ref[0] = jnp.concatenate(outs, axis=0).astype(o_ref.dtype)


def _swa(sink, bq, bk, bv, with_ctx):
    w = SWA_BLOCK
    n_blocks = SEQ // w
    n_lat = n_blocks // 2
    n_steps = (TOK if with_ctx else SEQ) // (2 * w)

    def edge_block(which):
        return pl.BlockSpec(
            (1, w, KV_WIDTH),
            lambda b, i: (b, jnp.clip(2 * i - 1 if which < 0 else 2 * i + 2, 0, n_blocks - 1), 0))

    own_spec = pl.BlockSpec((1, 2 * w, KV_WIDTH), lambda b, i: (b, jnp.minimum(i, n_lat - 1), 0))

    t = jnp.arange(GROUP * w)[:, None] % w
    kk = jnp.arange(3 * w)[None, :]
    band = jnp.where((kk - t >= 0) & (kk - t <= 2 * w), 0.0, -1e30).astype(F32)

    ctx_spec = pl.BlockSpec((1, CTX_LEN, KV_WIDTH), lambda b, i: (b, SEQ // CTX_LEN, 0))
    return pl.pallas_call(
        functools.partial(_swa_kernel, n_lat=n_lat),
        grid=(BATCH, n_steps),
        in_specs=[pl.BlockSpec(memory_space=pltpu.SMEM),
                  _const_spec((GROUP * w, 3 * w)),
                  pl.BlockSpec((1, 2 * w, Q_WIDTH), lambda b, i: (b, i, 0)),
                  ctx_spec, ctx_spec,
                  edge_block(-1), own_spec, edge_block(1),
                  edge_block(-1), own_spec, edge_block(1)],
        out_specs=pl.BlockSpec((1, 2 * w, Q_WIDTH), lambda b, i: (b, i, 0)),
        out_shape=jax.ShapeDtypeStruct((BATCH, n_steps * 2 * w, Q_WIDTH), BF16),
        compiler_params=_params("parallel", "parallel"),
        name="swa",
    )(sink, band, bq, bk, bv, bk, bk, bk, bv, bv, bv)


def _gqa_kernel(q_ref, k_ref, v_ref, *rest, ctx):
    o_ref = rest[-1]
    if ctx:
        keys, values = k_ref[0, SEQ:TOK, :], v_ref[0, SEQ:TOK, :]
    else:
        keys, values = k_ref[0], v_ref[0]
    tq = q_ref.shape[1]
    jobs = [_Job(slice(r, r + ATT_ROWS), keys, values, None) for r in range(0, tq, ATT_ROWS)]
    outs = _attend(q_ref, jobs, None, _EACH_GROUP, 1)
    o_ref[0] = (outs[0] if len(outs) == 1 else jnp.concatenate(outs, axis=0)).astype(o_ref.dtype)


def _gqa(cq, ck, cv, seg, rows, into):
    tile, first = seg.tile, seg.first
    kv_spec = pl.BlockSpec((1, TOK, KV_WIDTH), lambda b, i: (b, 0, 0))
    q_spec = pl.BlockSpec((1, tile, Q_WIDTH), lambda b, i: (b, first + i, 0))
    in_specs, args, aliases = [q_spec, kv_spec, kv_spec], [cq, ck, cv], {}
    if into is not None:
        in_specs.append(pl.BlockSpec(memory_space=pl.ANY))
        args.append(into)
        aliases = {3: 0}
    return pl.pallas_call(
        functools.partial(_gqa_kernel, ctx=seg.ctx),
        grid=(BATCH, seg.n_tiles),
        in_specs=in_specs,
        out_specs=q_spec,
        out_shape=jax.ShapeDtypeStruct((BATCH, rows, Q_WIDTH), BF16),
        input_output_aliases=aliases,
        compiler_params=_params("parallel", "parallel"),
        name="gqa_ctx" if seg.ctx else "gqa",
    )(*args)


def _out_proj_kernel(x_ref, of_ref, ob_ref, r_ref, b_ref, c_ref, mod_ref, w_ref, on_ref,
                     post_ref, pre_ref, x1_ref, h2_ref):
    ones = _seg_ones(LANES, GLA_DV)
    g1 = mod_ref[0, 0, 2:3, :]
    sh2 = mod_ref[0, 0, 3:4, :]
    sc2 = mod_ref[0, 0, 4:5, :]

    def part(rows):
        mix = _dot(b_ref[0, rows, :], w_ref[GLA_WIDTH:GLA_WIDTH + Q_WIDTH, :])
        mix = mix + _dot(c_ref[0, rows, :], w_ref[GLA_WIDTH + Q_WIDTH:, :])
        ya = []
        for lp in range(GLA_WIDTH // LANES):
            ls = slice(lp * LANES, (lp + 1) * LANES)
            o = of_ref[0, 0, rows, ls] + ob_ref[0, 0, rows, ls]
            ya.append((_head_rms(o, ones, on_ref[...]) * _silu(r_ref[0, rows, ls])).astype(BF16))
        mix = mix + _dot(jnp.concatenate(ya, axis=1), w_ref[0:GLA_WIDTH, :])
        yield
        x1 = x_ref[0, rows, :] + g1 * _row_rms(mix, post_ref[...])
        x1_ref[0, rows, :] = x1
        h2_ref[0, rows, :] = (_row_rms(x1, pre_ref[...]) * (1.0 + sc2) + sh2).astype(BF16)
        yield

    tile = x_ref.shape[1]
    _interleave([part(slice(r, r + ROW_PART)) for r in range(0, tile, ROW_PART)])


def _out_proj(x_seg, seg, o_f, o_b, gr, b_att, c_att, mod, layer, w, on, post, pre):
    tile, first = seg.tile, seg.first

    def own(width):
        return pl.BlockSpec((1, tile, width), lambda b, i: (b, i, 0))

    def combined(width):
        return pl.BlockSpec((1, tile, width), lambda b, i: (b, first + i, 0))

    gla_spec = pl.BlockSpec((1, 1, tile, GLA_WIDTH), lambda b, i: (0, b, first + i, 0))
    rows = seg.n_tiles * tile
    return pl.pallas_call(
        _out_proj_kernel,
        grid=(BATCH, seg.n_tiles),
        in_specs=[own(D_MODEL), gla_spec, gla_spec, combined(GLA_WIDTH),
                  combined(Q_WIDTH), combined(Q_WIDTH), _mod_spec(layer, seg),
                  _const_spec((D_MODEL, D_MODEL)), _const_spec((1, LANES)),
                  _const_spec((1, D_MODEL)), _const_spec((1, D_MODEL))],
        out_specs=[own(D_MODEL), own(D_MODEL)],
        out_shape=[jax.ShapeDtypeStruct((BATCH, rows, D_MODEL), F32),
                   jax.ShapeDtypeStruct((BATCH, rows, D_MODEL), BF16)],
        compiler_params=_params("parallel", "parallel"),
        name="out_proj_ctx" if seg.ctx else "out_proj",
    )(x_seg, o_f, o_b, gr, b_att, c_att, mod, w, on, post, pre)


def _ffn_kernel(hp_ref, h_ref, hn_ref, x1_ref, mod_ref, wu_ref, cw_ref, cb_ref, wd_ref, post_ref,
                o_ref, *, n_tiles):
    i = pl.program_id(1)
    tile = h_ref.shape[1]
    hp = jnp.where(i != 0, hp_ref[0], 0)
    hn = jnp.where(i != n_tiles - 1, hn_ref[0], 0)
    h_ext = jnp.concatenate([hp, h_ref[0], hn], axis=0)
    part = min(tile, FFN_PART)
    win = part + 2 * HALO
    mid = slice(HALO, HALO + part)

    def conv(u, c0):
        cw = cw_ref[:, c0:c0 + FFN_CHUNK]
        prev = pltpu.roll(u, 1, 0)[mid]
        nxt = pltpu.roll(u, win - 1, 0)[mid]
        return cw[0:1] * prev + cw[1:2] * u[mid] + cw[2:3] * nxt + cb_ref[:, c0:c0 + FFN_CHUNK]

    n_fc = FFN_DIM // FFN_CHUNK

    n_parts = tile // part
    cuts = [0] + [HALO + r * part for r in range(1, n_parts)] + [tile + 2 * HALO]
    lhs = [h_ext[cuts[r]:cuts[r + 1]] for r in range(n_parts)]

    def window(us, r):
        lo, hi = r * part, r * part + win
        pieces = []
        for q in range(n_parts):
            a, b = max(lo, cuts[q]), min(hi, cuts[q + 1])
            if a < b:
                pieces.append(us[q][a - cuts[q]:b - cuts[q]])
        return pieces[0] if len(pieces) == 1 else jnp.concatenate(pieces, axis=0)

    def up(fc):
        ca = fc * FFN_CHUNK
        cg = FFN_DIM + ca
        return ([_dot(x, wu_ref[:, ca:ca + FFN_CHUNK]) for x in lhs],
                [_dot(x, wu_ref[:, cg:cg + FFN_CHUNK]) for x in lhs])

    acc = [jnp.zeros((part, D_MODEL), F32) for _ in range(n_parts)]
    ua, ug = up(0)
    for fc in range(n_fc):
        nxt_u = up(fc + 1) if fc + 1 < n_fc else None
        ca = fc * FFN_CHUNK
        for r in range(n_parts):
            act = (_silu(conv(window(ua, r), ca)) * conv(window(ug, r), FFN_DIM + ca)).astype(BF16)
            acc[r] = acc[r] + _dot(act, wd_ref[ca:ca + FFN_CHUNK, :])
        if nxt_u is not None:
            ua, ug = nxt_u
    g2 = mod_ref[0, 0, 5:6, :]
    for r in range(tile // part):
        rows = slice(r * part, (r + 1) * part)
        o_ref[0, rows, :] = x1_ref[0, rows, :] + g2 * _row_rms(acc[r], post_ref[...])


def _ffn(h2, x1, seg, mod, layer, wu, cw, cb, wd, post):
    tile, n_tiles = seg.tile, seg.n_tiles
    per = tile // HALO
    n_halo = n_tiles * per

    def own(width):
        return pl.BlockSpec((1, tile, width), lambda b, i: (b, i, 0))

    prev_spec = pl.BlockSpec((1, HALO, D_MODEL), lambda b, i: (b, jnp.maximum(i * per - 1, 0), 0))
    next_spec = pl.BlockSpec((1, HALO, D_MODEL),
                             lambda b, i: (b, jnp.minimum((i + 1) * per, n_halo - 1), 0))
    return pl.pallas_call(
        functools.partial(_ffn_kernel, n_tiles=n_tiles),
        grid=(BATCH, n_tiles),
        in_specs=[prev_spec, own(D_MODEL), next_spec, own(D_MODEL), _mod_spec(layer, seg),
                  _const_spec((D_MODEL, 2 * FFN_DIM)), _const_spec((3, 2 * FFN_DIM)),
                  _const_spec((1, 2 * FFN_DIM)), _const_spec((FFN_DIM, D_MODEL)),
                  _const_spec((1, D_MODEL))],
        out_specs=own(D_MODEL),
        out_shape=jax.ShapeDtypeStruct((BATCH, n_tiles * tile, D_MODEL), F32),
        compiler_params=pltpu.CompilerParams(
            dimension_semantics=("parallel", "parallel"), vmem_limit_bytes=VMEM_LIMIT,
            ),
        name="ffn_ctx" if seg.ctx else "ffn",
    )(h2, h2, h2, x1, mod, wu, cw, cb, wd, post)


def _rope_tables():
    rows = SEQ // GRID_W
    row = jnp.repeat(jnp.arange(rows), GRID_W).astype(F32)
    col = (jnp.arange(rows * GRID_W) % GRID_W).astype(F32)
    n_freq = HEAD_DIM // 4
    inv_freq = ROPE_THETA ** (-jnp.arange(n_freq, dtype=F32) / n_freq)
    ang_r = row[:, None] * inv_freq[None, :]
    ang_c = col[:, None] * inv_freq[None, :]
    ang = jnp.concatenate([ang_r, ang_r, ang_c, ang_c], axis=-1)
    cos = jnp.tile(jnp.cos(ang), (1, LANES // HEAD_DIM))
    sin = jnp.tile(jnp.sin(ang), (1, LANES // HEAD_DIM))
    upper = (jnp.arange(LANES) % 32) < 16
    sin_up = jnp.where(upper[None, :], -sin, 0.0)
    sin_dn = jnp.where(upper[None, :], 0.0, sin)
    pad = jnp.zeros((CTX_LEN, LANES), F32)
    return (jnp.concatenate([cos, jnp.ones((CTX_LEN, LANES), F32)], axis=0),
            jnp.concatenate([sin_up, pad], axis=0), jnp.concatenate([sin_dn, pad], axis=0))


def _permute_heads(w, axis):
    heads = [lax.slice_in_dim(w, h * HEAD_DIM, (h + 1) * HEAD_DIM, axis=axis) for h in HEAD_PERM]
    return jnp.concatenate(heads, axis=axis)


def _layout_w_in(w):
    sizes = (GLA_QK, GLA_QK, GLA_WIDTH, GLA_WIDTH, 2 * GLA_GATE_RANK,
             Q_WIDTH, KV_WIDTH, KV_WIDTH, Q_WIDTH, KV_WIDTH, KV_WIDTH)
    w = w.astype(BF16)
    parts, start = [], 0
    for size in sizes:
        parts.append(w[:, start:start + size])
        start += size
    a_q, a_k, a_v, a_r, a_z, b_q, b_k, b_v, c_q, c_k, c_v = parts
    z_pad = jnp.zeros((w.shape[0], LANES - 2 * GLA_GATE_RANK), w.dtype)
    return jnp.concatenate(
        [a_q, a_k, a_v, a_r, _permute_heads(b_q, 1), _permute_heads(c_q, 1), b_k, b_v, c_k, c_v,
         a_z, z_pad], axis=1)


def _layout_w_out(w):
    w = w.astype(BF16)
    a = w[:GLA_WIDTH]
    b = _permute_heads(w[GLA_WIDTH:GLA_WIDTH + Q_WIDTH], 0)
    c = _permute_heads(w[GLA_WIDTH + Q_WIDTH:], 0)
    return jnp.concatenate([a, b, c], axis=0)


def _layout_gate(w_gate):
    out = jnp.zeros((2, LANES, GLA_QK), w_gate.dtype)
    for d in range(2):
        out = out.at[d, d * GLA_GATE_RANK:(d + 1) * GLA_GATE_RANK].set(w_gate[d])
    return out.astype(BF16)


def _diag_reduce_matrix():
    r = jnp.arange(GLA_SUB * LANES)
    c = jnp.arange(GLA_HEADS * GLA_SUB)
    same_s = (r[:, None] // LANES) == (c[None, :] % GLA_SUB)
    same_h = ((r[:, None] % LANES) // GLA_DK) == (c[None, :] // GLA_SUB)
    return jnp.where(same_s & same_h, 1.0, 0.0).astype(BF16)


def kernel(x, c, ctx, c_ctx, w_mod, b_mod, attn_pre_norm, attn_post_norm, ffn_pre_norm,
           ffn_post_norm, w_in, gla_w_gate, gla_b_gate, gla_out_norm, swa_sink, gqa_q_norm,
           gqa_k_norm, w_out, ffn_w_up, ffn_conv_w, ffn_conv_b, ffn_w_down):
    assert x.shape == (BATCH, SEQ, D_MODEL) and ctx.shape == (BATCH, CTX_LEN, D_MODEL)
    cos, sin_up, sin_dn = _rope_tables()
    wst = _diag_reduce_matrix()
    tri = _cumsum_matrices()
    cs = jnp.concatenate([c, c_ctx[None, :], jnp.zeros((16 - BATCH - 1, D_MODEL), F32)], axis=0)
    mod = _modulation(cs, w_mod, b_mod).reshape(DEPTH, 16, MOD_CHUNKS, D_MODEL)

    lat_proj, ctx_seg = _segment(False, PROJ_TILE), _segment(True, CTX_LEN)
    lat_ffn = _segment(False, FFN_TILE)
    lat_att = _segment(False, ATT_TQ)
    x_lat, x_ctx = x, ctx
    for layer in range(DEPTH):
        with_ctx = layer < DEPTH - 1
        row = lambda v: v[layer][None, :]
        tile2 = lambda v: jnp.tile(v[layer], LANES // HEAD_DIM)[None, :]
        proj_args = (mod, layer, row(attn_pre_norm), _layout_w_in(w_in[layer]), cos, sin_up, sin_dn,
                     tile2(gqa_q_norm), tile2(gqa_k_norm))
        proj = _in_proj(x_lat, lat_proj, None, *proj_args)
        gq, gk, gv, gr, gz, bq, bk, bv, cq, ck, cv = _in_proj(x_ctx, ctx_seg, proj, *proj_args)
        o_f, o_b = _gla(gq, gk, gv, gz, _layout_gate(gla_w_gate[layer]),
                        gla_b_gate[layer][:, None, :], tri, wst)
        b_att = _swa(swa_sink[layer].reshape(KV_HEADS, GROUP), bq, bk, bv, with_ctx)
        c_att = _gqa(cq, ck, cv, lat_att, TOK if with_ctx else SEQ, None)
        if with_ctx:
            c_att = _gqa(cq, ck, cv, ctx_seg, TOK, c_att)
        out_args = (o_f, o_b, gr, b_att, c_att, mod, layer, _layout_w_out(w_out[layer]),
                    tile2(gla_out_norm), row(attn_post_norm), row(ffn_pre_norm))
        ffn_args = (mod, layer, ffn_w_up[layer].astype(BF16), ffn_conv_w[layer], row(ffn_conv_b),
                    ffn_w_down[layer].astype(BF16), row(ffn_post_norm))
        x1, h2 = _out_proj(x_lat, lat_proj, *out_args)
        x_lat = _ffn(h2, x1, lat_ffn, *ffn_args)
        if with_ctx:
            x1, h2 = _out_proj(x_ctx, ctx_seg, *out_args)
            x_ctx = _ffn(h2, x1, ctx_seg, *ffn_args)
    return x_lat
```

```python
import functools
import itertools
from typing import NamedTuple

import jax
import jax.numpy as jnp
from jax import lax
from jax.experimental import pallas as pl
from jax.experimental.pallas import tpu as pltpu

F32 = jnp.float32
BF16 = jnp.bfloat16

D_MODEL = 1024
BATCH = 8
SEQ = 2048
DEPTH = 2
CTX_LEN = 256
TOK = SEQ + CTX_LEN
GRID_W = 64
HEAD_DIM = 64
ROPE_THETA = 10000.0
NORM_EPS = 1e-6
MOD_CHUNKS = 6

GLA_HEADS = 4
GLA_DK = 32
GLA_DV = 64
GLA_GATE_RANK = 16
GLA_GATE_TAU = 16.0
GLA_QK = GLA_HEADS * GLA_DK
GLA_WIDTH = GLA_HEADS * GLA_DV

Q_HEADS = 6
KV_HEADS = 2
GROUP = Q_HEADS // KV_HEADS
Q_WIDTH = Q_HEADS * HEAD_DIM
KV_WIDTH = KV_HEADS * HEAD_DIM
SWA_BLOCK = 128
SWA_TILE = 512

FFN_DIM = 2816
FFN_CHUNK = 256

LANES = 128
PROJ_TILE = 1024
ROW_PART = 256
FFN_TILE = 1024
FFN_PART = 256
HALO = 16
GLA_BLOCK = 256
GLA_CHUNK = 64
GLA_SUB = 16
GLA_BATCH = 2
ATT_TQ = 1024
ATT_ROWS = 256
VMEM_LIMIT = 56 * 1024 * 1024

C_GQ, C_GK, C_GV, C_GR = 0, 128, 256, 512
C_BQ, C_CQ = 768, 1152
C_BK, C_BV, C_CK, C_CV = 1536, 1664, 1792, 1920
C_GZ = 2048
IN_COLS = 2176
HEAD_PERM = (0, 3, 1, 4, 2, 5)

_NT = (((1,), (1,)), ((), ()))


class _Segment(NamedTuple):
    tile: int
    n_tiles: int
    first: int
    ctx: bool


def _segment(ctx, tile):
    if ctx:
        return _Segment(CTX_LEN, 1, SEQ // CTX_LEN, True)
    return _Segment(tile, SEQ // tile, 0, False)


def _dot(a, b):
    return jnp.dot(a, b, preferred_element_type=F32)


def _dot_nt(a, b):
    return lax.dot_general(a, b, _NT, preferred_element_type=F32)


def _idiv(x, n):
    assert n & (n - 1) == 0
    return lax.shift_right_logical(x, jnp.int32(n.bit_length() - 1))


def _split2(x):
    hi = x.astype(BF16)
    lo = (x - hi.astype(F32)).astype(BF16)
    return hi, lo


def _split3(x):
    h1 = x.astype(BF16)
    r1 = x - h1.astype(F32)
    h2 = r1.astype(BF16)
    h3 = (r1 - h2.astype(F32)).astype(BF16)
    return h1, h2, h3


def _seg_ones(width, seg):
    r = _idiv(lax.broadcasted_iota(jnp.int32, (width, width), 0), seg)
    c = _idiv(lax.broadcasted_iota(jnp.int32, (width, width), 1), seg)
    return jnp.where(r == c, 1.0, 0.0).astype(BF16)


def _seg_sum(x, ones):
    hi, lo = _split2(x)
    return _dot(hi, ones) + _dot(lo, ones)


def _row_rms(x, w):
    ms = jnp.mean(x * x, axis=-1, keepdims=True)
    return x * lax.rsqrt(ms + NORM_EPS) * w


def _head_rms(x, ones, w):
    ss = _seg_sum(x * x, ones)
    return x * lax.rsqrt(ss * (1.0 / HEAD_DIM) + NORM_EPS) * w


def _silu(x):
    return x / (1.0 + jnp.exp(-x))


def _rope(x, cos, sin_up, sin_dn):
    return x * cos + pltpu.roll(x, LANES - 16, 1) * sin_up + pltpu.roll(x, 16, 1) * sin_dn


def _const_spec(shape):
    return pl.BlockSpec(shape, lambda b, i: (0,) * len(shape))


def _params(*semantics):
    return pltpu.CompilerParams(dimension_semantics=semantics, vmem_limit_bytes=VMEM_LIMIT)


def _mod_kernel(c_ref, w_ref, b_ref, o_ref):
    a = _silu(c_ref[...]).astype(BF16)
    o_ref[0] = _dot(a, w_ref[0].astype(BF16)) + b_ref[0]


def _modulation(cs, w_mod, b_mod):
    tn = 1536
    n = MOD_CHUNKS * D_MODEL
    return pl.pallas_call(
        _mod_kernel,
        grid=(DEPTH, n // tn),
        in_specs=[
            pl.BlockSpec((16, D_MODEL), lambda l, j: (0, 0)),
            pl.BlockSpec((1, D_MODEL, tn), lambda l, j: (l, 0, j)),
            pl.BlockSpec((1, 1, tn), lambda l, j: (l, 0, j)),
        ],
        out_specs=pl.BlockSpec((1, 16, tn), lambda l, j: (l, 0, j)),
        out_shape=jax.ShapeDtypeStruct((DEPTH, 16, n), F32),
        compiler_params=_params("arbitrary", "arbitrary"),
        name="modulation",
    )(cs, w_mod, b_mod.reshape(DEPTH, 1, n))


def _mod_spec(layer, seg):
    return pl.BlockSpec((1, 1, MOD_CHUNKS, D_MODEL),
                        lambda b, i: (layer, BATCH if seg.ctx else b, 0, 0))


_PROJ_OUTPUTS = ((GLA_QK, F32), (GLA_QK, F32), (GLA_WIDTH, F32), (GLA_WIDTH, F32), (LANES, F32),
                 (Q_WIDTH, BF16), (KV_WIDTH, BF16), (KV_WIDTH, BF16),
                 (Q_WIDTH, BF16), (KV_WIDTH, BF16), (KV_WIDTH, BF16))


def _interleave(streams):
    for _ in itertools.zip_longest(*streams):
        pass


def _lag(stream, stages):
    for _ in range(stages):
        yield
    yield from stream


def _in_proj_kernel(x_ref, mod_ref, nw_ref, w_ref, cos_ref, su_ref, sd_ref, qn_ref, kn_ref, *rest):
    (gq_ref, gk_ref, gv_ref, gr_ref, gz_ref, bq_ref, bk_ref, bv_ref,
     cq_ref, ck_ref, cv_ref) = rest[-len(_PROJ_OUTPUTS):]
    sh = mod_ref[0, 0, 0:1, :]
    sc = mod_ref[0, 0, 1:2, :]
    ones = _seg_ones(LANES, HEAD_DIM)
    att_scale = HEAD_DIM ** -0.5

    def part(rows):
        h = (_row_rms(x_ref[0, rows, :], nw_ref[...]) * (1.0 + sc) + sh).astype(BF16)

        def proj(c0, width):
            return _dot(h, w_ref[:, c0:c0 + width])

        cos, su, sd = cos_ref[rows, :], su_ref[rows, :], sd_ref[rows, :]
        p_a = proj(C_GQ, C_BQ - C_GQ)
        p_q = proj(C_BQ, C_BK - C_BQ)
        yield
        gq_ref[0, rows, :] = p_a[:, C_GQ:C_GQ + GLA_QK] * (GLA_DK ** -0.5)
        gk_ref[0, rows, :] = p_a[:, C_GK:C_GK + GLA_QK]
        gv_ref[0, rows, :] = p_a[:, C_GV:C_GV + GLA_WIDTH]
        gr_ref[0, rows, :] = p_a[:, C_GR:C_GR + GLA_WIDTH]
        p_kv = proj(C_BK, IN_COLS - C_BK)

        def kv(c0):
            return p_kv[:, c0 - C_BK:c0 - C_BK + KV_WIDTH]
        yield
        for g in range(GROUP):
            ls = slice(g * LANES, (g + 1) * LANES)
            bq_ref[0, rows, ls] = (_rope(p_q[:, ls], cos, su, sd) * att_scale).astype(BF16)
            cq = _head_rms(p_q[:, Q_WIDTH + g * LANES:Q_WIDTH + (g + 1) * LANES], ones, qn_ref[...])
            cq_ref[0, rows, ls] = (_rope(cq, cos, su, sd) * att_scale).astype(BF16)
        yield
        bk_ref[0, rows, :] = _rope(kv(C_BK), cos, su, sd).astype(BF16)
        bv_ref[0, rows, :] = kv(C_BV).astype(BF16)
        ck_ref[0, rows, :] = _rope(_head_rms(kv(C_CK), ones, kn_ref[...]), cos, su, sd).astype(BF16)
        cv_ref[0, rows, :] = kv(C_CV).astype(BF16)
        gz_ref[0, rows, :] = kv(C_GZ)
        yield

    tile = x_ref.shape[1]
    _interleave([part(slice(r, r + ROW_PART)) for r in range(0, tile, ROW_PART)])


def _in_proj(x_seg, seg, into, mod, layer, nw, w, cos, su, sd, qn, kn):
    tile, first = seg.tile, seg.first

    def rows(width):
        return pl.BlockSpec((1, tile, width), lambda b, i: (b, first + i, 0))

    def table():
        return pl.BlockSpec((tile, LANES), lambda b, i: (first + i, 0))

    in_specs = [pl.BlockSpec((1, tile, D_MODEL), lambda b, i: (b, i, 0)), _mod_spec(layer, seg),
                _const_spec((1, D_MODEL)), _const_spec((D_MODEL, IN_COLS)),
                table(), table(), table(), _const_spec((1, LANES)), _const_spec((1, LANES))]
    args = [x_seg, mod, nw, w, cos, su, sd, qn, kn]
    aliases = {}
    if into is not None:
        aliases = {len(args) + k: k for k in range(len(into))}
        in_specs += [pl.BlockSpec(memory_space=pl.ANY)] * len(into)
        args += list(into)
    return pl.pallas_call(
        _in_proj_kernel,
        grid=(BATCH, seg.n_tiles),
        in_specs=in_specs,
        out_specs=[rows(wd) for wd, _ in _PROJ_OUTPUTS],
        out_shape=[jax.ShapeDtypeStruct((BATCH, TOK, wd), dt) for wd, dt in _PROJ_OUTPUTS],
        input_output_aliases=aliases,
        compiler_params=_params("parallel", "parallel"),
        name="in_proj_ctx" if seg.ctx else "in_proj",
    )(*args)


def _lane_group_mask(width, group, h):
    lane = lax.broadcasted_iota(jnp.int32, (1, width), 1)
    return _idiv(lane, group) == h


def _gla_direction(q_ref, k_ref, v_ref, z_ref, wg, bg, tri, wst_ref, o_ref, st_ref, b_s, e_s, rev):
    nb, c_len, sb = GLA_BLOCK, GLA_CHUNK, GLA_SUB
    n_chunks, n_sub = nb // c_len, c_len // sb

    g = _dot(z_ref[0].astype(BF16), wg) + bg
    la = (jnp.minimum(g, 0.0) - jnp.log1p(jnp.exp(-jnp.abs(g)))) * (1.0 / GLA_GATE_TAU)

    h1, h2, h3 = _split3(la)
    b = _dot(tri, h1) + _dot(tri, h2) + _dot(tri, h3)
    b_s[...] = b
    yield

    q, k, v = q_ref[0], k_ref[0], v_ref[0]

    half = sb // 2
    t_half = lax.broadcasted_iota(jnp.int32, (half, 1), 0)

    for jb in range(nb // sb):
        r0 = jb * sb
        bb = [b_s[pl.ds(r0 + lo, half), :] for lo in (0, half)]
        qb = [q_ref[0, pl.ds(r0 + lo, half), :] for lo in (0, half)]
        for s in range(sb):
            bs = b_s[pl.ds(r0 + s, 1), :]
            ks = k_ref[0, pl.ds(r0 + s, 1), :]
            halves = []
            for hi, lo in enumerate((0, half)):
                none = (lo > s) if rev else (lo + half - 1 < s)
                every = (lo + half - 1 <= s) if rev else (lo >= s)
                if none:
                    halves.append(jnp.zeros((half, LANES), F32))
                    continue
                diff = bb[hi] - bs
                if every:
                    e = qb[hi] * ks * jnp.exp(diff)
                else:
                    keep = (t_half + lo <= s) if rev else (t_half + lo >= s)
                    e = jnp.where(keep, qb[hi] * ks * jnp.exp(jnp.minimum(diff, 0.0)), 0.0)
                halves.append(e)
            e_s[pl.ds(r0, sb), s * LANES:(s + 1) * LANES] = jnp.concatenate(
                halves, axis=0).astype(BF16)
        if jb % 4 == 3:
            yield

    a_diag = _dot(e_s[...], wst_ref[...])
    yield

    hm_k = [_lane_group_mask(GLA_QK, GLA_DK, h) for h in range(GLA_HEADS)]
    hm_v = [_lane_group_mask(GLA_WIDTH, GLA_DV, h) for h in range(GLA_HEADS)]
    st_row = _idiv(lax.broadcasted_iota(jnp.int32, (GLA_WIDTH, GLA_QK), 0), GLA_DV)
    st_col = _idiv(lax.broadcasted_iota(jnp.int32, (GLA_WIDTH, GLA_QK), 1), GLA_DK)
    st_mask = st_row == st_col
    v_t = jnp.transpose(v).astype(BF16)
    chunk = [slice(c * c_len, (c + 1) * c_len) for c in range(n_chunks)]
    sub = [slice(j * sb, (j + 1) * sb) for j in range(n_sub)]

    q_hat, kv_t, d_row = [], [], []
    for c in range(n_chunks):
        bc = b[chunk[c]]
        end = 0 if rev else c_len - 1
        b_end = bc[end:end + 1]
        q_hat.append((q[chunk[c]] * jnp.exp(bc)).astype(BF16))
        d_row.append(jnp.exp(b_end))
        k_hat = (k[chunk[c]] * jnp.exp(b_end - bc)).astype(BF16)
        pads = [jnp.zeros((n, GLA_QK), BF16) for n in (c * c_len, nb - (c + 1) * c_len) if n]
        k_hat = jnp.concatenate(pads[:1 if c else 0] + [k_hat] + pads[1 if c else 0:], axis=0)
        kv_t.append(jnp.where(st_mask, _dot(v_t, k_hat), 0.0))
    yield

    scores = {}
    for c in range(n_chunks):
        bc, qc, kc = b[chunk[c]], q[chunk[c]], k[chunk[c]]
        for j in range(n_sub):
            if not ((j > 0) if rev else (j < n_sub - 1)):
                continue
            edge = j * sb if rev else (j + 1) * sb - 1
            ref_b = bc[edge:edge + 1]
            later = slice(0, j * sb) if rev else slice((j + 1) * sb, c_len)
            q_l = (qc[later] * jnp.exp(bc[later] - ref_b)).astype(BF16)
            k_j = kc[sub[j]] * jnp.exp(ref_b - bc[sub[j]])
            k_tile = jnp.concatenate(
                [jnp.where(hm_k[h], k_j, 0.0) for h in range(GLA_HEADS)], axis=0).astype(BF16)
            scores[(c, j)] = _dot_nt(q_l, k_tile)
        yield

    o_intra = []
    for c in range(n_chunks):
        vc = v[chunk[c]]
        blocks = [None] * n_sub
        for j in range(n_sub):
            v_exp = jnp.concatenate(
                [jnp.where(hm_v[h], vc[sub[j]], 0.0) for h in range(GLA_HEADS)],
                axis=0).astype(BF16)
            ad = a_diag[c * c_len + j * sb:c * c_len + (j + 1) * sb]
            if (c, j) in scores:
                sc = scores[(c, j)]
                p = jnp.concatenate([sc, ad], axis=0) if rev else jnp.concatenate([ad, sc], axis=0)
                first = 0 if rev else j
            else:
                p, first = ad, j
            contrib = _dot(p.astype(BF16), v_exp)
            for r in range(contrib.shape[0] // sb):
                piece = contrib[r * sb:(r + 1) * sb]
                blocks[first + r] = piece if blocks[first + r] is None else blocks[first + r] + piece
        o_intra.append(jnp.concatenate(blocks, axis=0))
        yield

    state = st_ref[...]
    outs = [None] * n_chunks
    for c in (reversed(range(n_chunks)) if rev else range(n_chunks)):
        outs[c] = o_intra[c] + _dot_nt(q_hat[c], state.astype(BF16))
        state = state * d_row[c] + kv_t[c]
        yield
    st_ref[...] = state
    o_ref[0, 0] = jnp.concatenate(outs, axis=0)


def _gla_kernel(qf, kf, vf, zf, qb, kb, vb, zb, wg_ref, bg_ref, tri_ref, wst_ref, of_ref, ob_ref,
                sf_ref, sb_ref, bf_s, bb_s, ef_s, eb_s):
    @pl.when(pl.program_id(1) == 0)
    def _():
        sf_ref[...] = jnp.zeros_like(sf_ref)
        sb_ref[...] = jnp.zeros_like(sb_ref)

    streams = []
    for bi in range(GLA_BATCH):
        one = pl.ds(bi, 1)
        streams.append(_gla_direction(
            qf.at[one], kf.at[one], vf.at[one], zf.at[one], wg_ref[0], bg_ref[0], tri_ref[0],
            wst_ref, of_ref.at[:, one], sf_ref.at[bi], bf_s.at[bi], ef_s.at[bi], False))
        streams.append(_gla_direction(
            qb.at[one], kb.at[one], vb.at[one], zb.at[one], wg_ref[1], bg_ref[1], tri_ref[1],
            wst_ref, ob_ref.at[:, one], sb_ref.at[bi], bb_s.at[bi], eb_s.at[bi], True))
    _interleave([_lag(s, n) for n, s in enumerate(streams)])


def _cumsum_matrices():
    t = jnp.arange(GLA_BLOCK)[:, None]
    s = jnp.arange(GLA_BLOCK)[None, :]
    inside = (t // GLA_CHUNK) == (s // GLA_CHUNK)
    return jnp.stack([jnp.where(inside & (s <= t), 1.0, 0.0),
                      jnp.where(inside & (s >= t), 1.0, 0.0)]).astype(BF16)


def _gla(gq, gk, gv, gz, wg, bg, tri, wst):
    n_lat = SEQ // GLA_BLOCK
    ctx_blk = n_lat

    def fwd(b, s):
        return (b, jnp.where(s == 0, ctx_blk, s - 1), 0)

    def bwd(b, s):
        return (b, jnp.where(s == 0, ctx_blk, n_lat - s), 0)

    def specs(index_map):
        return [pl.BlockSpec((GLA_BATCH, GLA_BLOCK, wd), index_map)
                for wd in (GLA_QK, GLA_QK, GLA_WIDTH, LANES)]

    return pl.pallas_call(
        _gla_kernel,
        grid=(BATCH // GLA_BATCH, n_lat + 1),
        in_specs=specs(fwd) + specs(bwd) + [
            _const_spec((2, LANES, GLA_QK)), _const_spec((2, 1, GLA_QK)),
            _const_spec((2, GLA_BLOCK, GLA_BLOCK)),
            _const_spec((GLA_SUB * LANES, GLA_HEADS * GLA_SUB))],
        out_specs=[pl.BlockSpec((1, GLA_BATCH, GLA_BLOCK, GLA_WIDTH), lambda b, s: (0,) + fwd(b, s)),
                   pl.BlockSpec((1, GLA_BATCH, GLA_BLOCK, GLA_WIDTH), lambda b, s: (0,) + bwd(b, s))],
        out_shape=[jax.ShapeDtypeStruct((1, BATCH, TOK, GLA_WIDTH), F32)] * 2,
        scratch_shapes=[pltpu.VMEM((GLA_BATCH, GLA_WIDTH, GLA_QK), F32)] * 2
        + [pltpu.VMEM((GLA_BATCH, GLA_BLOCK, GLA_QK), F32)] * 2
        + [pltpu.VMEM((GLA_BATCH, GLA_BLOCK, GLA_SUB * LANES), BF16)] * 2,
        compiler_params=_params("parallel", "arbitrary"),
        name="gla",
    )(gq, gk, gv, gz, gq, gk, gv, gz, wg, bg, tri, wst)


def _fold_lanes(x, op):
    acc = x[:, 0:LANES]
    for c in range(1, x.shape[1] // LANES):
        acc = op(acc, x[:, c * LANES:(c + 1) * LANES])
    return acc


class _Job(NamedTuple):
    rows: slice
    keys: jax.Array
    values: jax.Array
    bias: jax.Array


def _attend(q_ref, jobs, sink_ref, groups, ahead):
    lane = lax.broadcasted_iota(jnp.int32, (1, LANES), 1)
    first = lane < HEAD_DIM
    units = [(jb, h, gs) for jb in range(len(jobs)) for gs in groups for h in range(KV_HEADS)]

    def scores(jb, h, gs):
        mine = first if h == 0 else jnp.logical_not(first)
        qz = [jnp.where(mine, q_ref[0, jobs[jb].rows, g * LANES:(g + 1) * LANES], 0) for g in gs]
        qz = qz[0] if len(gs) == 1 else jnp.concatenate(qz, axis=0)
        return _dot_nt(qz, jobs[jb].keys)

    pending = {}
    for idx in range(min(ahead, len(units))):
        pending[idx] = scores(*units[idx])
    outs = {}
    for idx, (jb, h, gs) in enumerate(units):
        if idx + ahead < len(units):
            pending[idx + ahead] = scores(*units[idx + ahead])
        s = pending.pop(idx)
        job = jobs[jb]
        tq = job.rows.stop - job.rows.start
        if job.bias is not None:
            nb = job.bias.shape[1]
            s = jnp.concatenate([s[:, :nb] + job.bias[:s.shape[0]], s[:, nb:]], axis=1)
        m = jnp.max(_fold_lanes(s, jnp.maximum), axis=-1, keepdims=True)
        if sink_ref is not None:
            sk = [jnp.full((tq, 1), sink_ref[h, g], F32) for g in gs]
            sk = sk[0] if len(gs) == 1 else jnp.concatenate(sk, axis=0)
            m = jnp.maximum(m, sk)
        p = jnp.exp(s - m)
        den = jnp.sum(_fold_lanes(p, jnp.add), axis=-1, keepdims=True)
        if sink_ref is not None:
            den = den + jnp.exp(sk - m)
        o = _dot(p.astype(BF16), job.values) / den
        for r, g in enumerate(gs):
            outs[(jb, h, g)] = o[r * tq:(r + 1) * tq]
    return [jnp.concatenate([jnp.where(first, outs[(jb, 0, g)], outs[(jb, 1, g)])
                             for g in range(GROUP)], axis=1) for jb in range(len(jobs))]


_ALL_GROUPS = (tuple(range(GROUP)),)
_EACH_GROUP = tuple((g,) for g in range(GROUP))


def _swa_ctx_kernel(sink_ref, q_ref, kc_ref, vc_ref, _, o_ref):
    job = _Job(slice(0, q_ref.shape[1]), kc_ref[0], vc_ref[0], None)
    o_ref[0] = _attend(q_ref, [job], sink_ref, _ALL_GROUPS, 2)[0].astype(o_ref.dtype)


def _swa_kernel(sink_ref, band_ref, q_ref, kc_ref, vc_ref, kp_ref, kk_ref, kn_ref, vp_ref, vk_ref,
                vn_ref, o_ref, *, n_steps):
    i = pl.program_id(1)
    w = SWA_BLOCK
    n = q_ref.shape[1] // w
    local_k = jnp.concatenate([kp_ref[0], kk_ref[0], kn_ref[0]], axis=0)
    local_v = jnp.concatenate([vp_ref[0], vk_ref[0], vn_ref[0]], axis=0)
    col = lax.broadcasted_iota(jnp.int32, (1, 3 * w), 1)
    jobs = []
    for g in range(n):
        bias = band_ref[...]
        if g == 0:
            bias = bias + jnp.where((col < w) & (i == 0), -1e30, 0.0)
        if g == n - 1:
            bias = bias + jnp.where((col >= 2 * w) & (i == n_steps - 1), -1e30, 0.0)
        rows = slice(g * w, (g + 3) * w)
        jobs.append(_Job(slice(g * w, (g + 1) * w),
                         jnp.concatenate([local_k[rows], kc_ref[0]], axis=0),
                         jnp.concatenate([local_v[rows], vc_ref[0]], axis=0), bias))
    outs = _attend(q_ref, jobs, sink_ref, _ALL_GROUPS, 4)
    o_ref[0] = jnp.concatenate(outs, axis=0).astype(o_ref.dtype)


def _swa(sink, bq, bk, bv, with_ctx):
    w = SWA_BLOCK
    n_blocks = SEQ // w
    per = SWA_TILE // w
    n_steps = SEQ // SWA_TILE
    rows = TOK if with_ctx else SEQ

    def edge_block(which):
        return pl.BlockSpec(
            (1, w, KV_WIDTH),
            lambda b, i: (b, jnp.clip(per * i - 1 if which < 0 else per * (i + 1), 0, n_blocks - 1), 0))

    own_spec = pl.BlockSpec((1, SWA_TILE, KV_WIDTH), lambda b, i: (b, i, 0))
    q_spec = pl.BlockSpec((1, SWA_TILE, Q_WIDTH), lambda b, i: (b, i, 0))
    ctx_spec = pl.BlockSpec((1, CTX_LEN, KV_WIDTH), lambda b, i: (b, SEQ // CTX_LEN, 0))
    sink_spec = pl.BlockSpec(memory_space=pltpu.SMEM)

    t = jnp.arange(GROUP * w)[:, None] % w
    kk = jnp.arange(3 * w)[None, :]
    band = jnp.where((kk - t >= 0) & (kk - t <= 2 * w), 0.0, -1e30).astype(F32)

    out = pl.pallas_call(
        functools.partial(_swa_kernel, n_steps=n_steps),
        grid=(BATCH, n_steps),
        in_specs=[sink_spec, _const_spec((GROUP * w, 3 * w)), q_spec, ctx_spec, ctx_spec,
                  edge_block(-1), own_spec, edge_block(1),
                  edge_block(-1), own_spec, edge_block(1)],
        out_specs=q_spec,
        out_shape=jax.ShapeDtypeStruct((BATCH, rows, Q_WIDTH), BF16),
        compiler_params=_params("parallel", "parallel"),
        name="swa",
    )(sink, band, bq, bk, bv, bk, bk, bk, bv, bv, bv)
    if not with_ctx:
        return out
    ctx_q_spec = pl.BlockSpec((1, CTX_LEN, Q_WIDTH), lambda b, i: (b, SEQ // CTX_LEN, 0))
    return pl.pallas_call(
        _swa_ctx_kernel,
        grid=(BATCH, 1),
        in_specs=[sink_spec, ctx_q_spec, ctx_spec, ctx_spec, pl.BlockSpec(memory_space=pl.ANY)],
        out_specs=ctx_q_spec,
        out_shape=jax.ShapeDtypeStruct((BATCH, rows, Q_WIDTH), BF16),
        input_output_aliases={4: 0},
        compiler_params=_params("parallel", "parallel"),
        name="swa_ctx",
    )(sink, bq, bk, bv, out)


def _gqa_kernel(q_ref, k_ref, v_ref, *rest, ctx):
    o_ref = rest[-1]
    if ctx:
        keys, values = k_ref[0, SEQ:TOK, :], v_ref[0, SEQ:TOK, :]
    else:
        keys, values = k_ref[0], v_ref[0]
    tq = q_ref.shape[1]
    jobs = [_Job(slice(r, r + ATT_ROWS), keys, values, None) for r in range(0, tq, ATT_ROWS)]
    outs = _attend(q_ref, jobs, None, _EACH_GROUP, 1)
    o_ref[0] = (outs[0] if len(outs) == 1 else jnp.concatenate(outs, axis=0)).astype(o_ref.dtype)


def _gqa(cq, ck, cv, seg, rows, into):
    tile, first = seg.tile, seg.first
    kv_spec = pl.BlockSpec((1, TOK, KV_WIDTH), lambda b, i: (b, 0, 0))
    q_spec = pl.BlockSpec((1, tile, Q_WIDTH), lambda b, i: (b, first + i, 0))
    in_specs, args, aliases = [q_spec, kv_spec, kv_spec], [cq, ck, cv], {}
    if into is not None:
        in_specs.append(pl.BlockSpec(memory_space=pl.ANY))
        args.append(into)
        aliases = {3: 0}
    return pl.pallas_call(
        functools.partial(_gqa_kernel, ctx=seg.ctx),
        grid=(BATCH, seg.n_tiles),
        in_specs=in_specs,
        out_specs=q_spec,
        out_shape=jax.ShapeDtypeStruct((BATCH, rows, Q_WIDTH), BF16),
        input_output_aliases=aliases,
        compiler_params=_params("parallel", "parallel"),
        name="gqa_ctx" if seg.ctx else "gqa",
    )(*args)


def _out_proj_kernel(x_ref, of_ref, ob_ref, r_ref, b_ref, c_ref, mod_ref, w_ref, on_ref,
                     post_ref, pre_ref, x1_ref, h2_ref):
    ones = _seg_ones(LANES, GLA_DV)
    g1 = mod_ref[0, 0, 2:3, :]
    sh2 = mod_ref[0, 0, 3:4, :]
    sc2 = mod_ref[0, 0, 4:5, :]

    def part(rows):
        mix = _dot(b_ref[0, rows, :], w_ref[GLA_WIDTH:GLA_WIDTH + Q_WIDTH, :])
        mix = mix + _dot(c_ref[0, rows, :], w_ref[GLA_WIDTH + Q_WIDTH:, :])
        ya = []
        for lp in range(GLA_WIDTH // LANES):
            ls = slice(lp * LANES, (lp + 1) * LANES)
            o = of_ref[0, 0, rows, ls] + ob_ref[0, 0, rows, ls]
            ya.append((_head_rms(o, ones, on_ref[...]) * _silu(r_ref[0, rows, ls])).astype(BF16))
        mix = mix + _dot(jnp.concatenate(ya, axis=1), w_ref[0:GLA_WIDTH, :])
        yield
        x1 = x_ref[0, rows, :] + g1 * _row_rms(mix, post_ref[...])
        x1_ref[0, rows, :] = x1
        h2_ref[0, rows, :] = (_row_rms(x1, pre_ref[...]) * (1.0 + sc2) + sh2).astype(BF16)
        yield

    tile = x_ref.shape[1]
    _interleave([part(slice(r, r + ROW_PART)) for r in range(0, tile, ROW_PART)])


def _out_proj(x_seg, seg, o_f, o_b, gr, b_att, c_att, mod, layer, w, on, post, pre):
    tile, first = seg.tile, seg.first

    def own(width):
        return pl.BlockSpec((1, tile, width), lambda b, i: (b, i, 0))

    def combined(width):
        return pl.BlockSpec((1, tile, width), lambda b, i: (b, first + i, 0))

    gla_spec = pl.BlockSpec((1, 1, tile, GLA_WIDTH), lambda b, i: (0, b, first + i, 0))
    rows = seg.n_tiles * tile
    return pl.pallas_call(
        _out_proj_kernel,
        grid=(BATCH, seg.n_tiles),
        in_specs=[own(D_MODEL), gla_spec, gla_spec, combined(GLA_WIDTH),
                  combined(Q_WIDTH), combined(Q_WIDTH), _mod_spec(layer, seg),
                  _const_spec((D_MODEL, D_MODEL)), _const_spec((1, LANES)),
                  _const_spec((1, D_MODEL)), _const_spec((1, D_MODEL))],
        out_specs=[own(D_MODEL), own(D_MODEL)],
        out_shape=[jax.ShapeDtypeStruct((BATCH, rows, D_MODEL), F32),
                   jax.ShapeDtypeStruct((BATCH, rows, D_MODEL), BF16)],
        compiler_params=_params("parallel", "parallel"),
        name="out_proj_ctx" if seg.ctx else "out_proj",
    )(x_seg, o_f, o_b, gr, b_att, c_att, mod, w, on, post, pre)


def _ffn_kernel(hp_ref, h_ref, hn_ref, x1_ref, mod_ref, wu_ref, cw_ref, cb_ref, wd_ref, post_ref,
                o_ref, *, n_tiles):
    i = pl.program_id(1)
    tile = h_ref.shape[1]
    hp = jnp.where(i != 0, hp_ref[0], 0)
    hn = jnp.where(i != n_tiles - 1, hn_ref[0], 0)
    h_ext = jnp.concatenate([hp, h_ref[0], hn], axis=0)
    part = min(tile, FFN_PART)
    win = part + 2 * HALO
    mid = slice(HALO, HALO + part)

    def conv(u, c0):
        cw = cw_ref[:, c0:c0 + FFN_CHUNK]
        prev = pltpu.roll(u, 1, 0)[mid]
        nxt = pltpu.roll(u, win - 1, 0)[mid]
        return cw[0:1] * prev + cw[1:2] * u[mid] + cw[2:3] * nxt + cb_ref[:, c0:c0 + FFN_CHUNK]

    n_fc = FFN_DIM // FFN_CHUNK

    n_parts = tile // part
    cuts = [0] + [HALO + r * part for r in range(1, n_parts)] + [tile + 2 * HALO]
    lhs = [h_ext[cuts[r]:cuts[r + 1]] for r in range(n_parts)]

    def window(us, r):
        lo, hi = r * part, r * part + win
        pieces = []
        for q in range(n_parts):
            a, b = max(lo, cuts[q]), min(hi, cuts[q + 1])
            if a < b:
                pieces.append(us[q][a - cuts[q]:b - cuts[q]])
        return pieces[0] if len(pieces) == 1 else jnp.concatenate(pieces, axis=0)

    def up(fc):
        ca = fc * FFN_CHUNK
        cg = FFN_DIM + ca
        return ([_dot(x, wu_ref[:, ca:ca + FFN_CHUNK]) for x in lhs],
                [_dot(x, wu_ref[:, cg:cg + FFN_CHUNK]) for x in lhs])

    acc = [jnp.zeros((part, D_MODEL), F32) for _ in range(n_parts)]
    ua, ug = up(0)
    for fc in range(n_fc):
        nxt_u = up(fc + 1) if fc + 1 < n_fc else None
        ca = fc * FFN_CHUNK
        for r in range(n_parts):
            act = (_silu(conv(window(ua, r), ca)) * conv(window(ug, r), FFN_DIM + ca)).astype(BF16)
            acc[r] = acc[r] + _dot(act, wd_ref[ca:ca + FFN_CHUNK, :])
        if nxt_u is not None:
            ua, ug = nxt_u
    g2 = mod_ref[0, 0, 5:6, :]
    for r in range(tile // part):
        rows = slice(r * part, (r + 1) * part)
        o_ref[0, rows, :] = x1_ref[0, rows, :] + g2 * _row_rms(acc[r], post_ref[...])


def _ffn(h2, x1, seg, mod, layer, wu, cw, cb, wd, post):
    tile, n_tiles = seg.tile, seg.n_tiles
    per = tile // HALO
    n_halo = n_tiles * per

    def own(width):
        return pl.BlockSpec((1, tile, width), lambda b, i: (b, i, 0))

    prev_spec = pl.BlockSpec((1, HALO, D_MODEL), lambda b, i: (b, jnp.maximum(i * per - 1, 0), 0))
    next_spec = pl.BlockSpec((1, HALO, D_MODEL),
                             lambda b, i: (b, jnp.minimum((i + 1) * per, n_halo - 1), 0))
    return pl.pallas_call(
        functools.partial(_ffn_kernel, n_tiles=n_tiles),
        grid=(BATCH, n_tiles),
        in_specs=[prev_spec, own(D_MODEL), next_spec, own(D_MODEL), _mod_spec(layer, seg),
                  _const_spec((D_MODEL, 2 * FFN_DIM)), _const_spec((3, 2 * FFN_DIM)),
                  _const_spec((1, 2 * FFN_DIM)), _const_spec((FFN_DIM, D_MODEL)),
                  _const_spec((1, D_MODEL))],
        out_specs=own(D_MODEL),
        out_shape=jax.ShapeDtypeStruct((BATCH, n_tiles * tile, D_MODEL), F32),
        compiler_params=pltpu.CompilerParams(
            dimension_semantics=("parallel", "parallel"), vmem_limit_bytes=VMEM_LIMIT,
            ),
        name="ffn_ctx" if seg.ctx else "ffn",
    )(h2, h2, h2, x1, mod, wu, cw, cb, wd, post)


def _rope_tables():
    rows = SEQ // GRID_W
    row = jnp.repeat(jnp.arange(rows), GRID_W).astype(F32)
    col = (jnp.arange(rows * GRID_W) % GRID_W).astype(F32)
    n_freq = HEAD_DIM // 4
    inv_freq = ROPE_THETA ** (-jnp.arange(n_freq, dtype=F32) / n_freq)
    ang_r = row[:, None] * inv_freq[None, :]
    ang_c = col[:, None] * inv_freq[None, :]
    ang = jnp.concatenate([ang_r, ang_r, ang_c, ang_c], axis=-1)
    cos = jnp.tile(jnp.cos(ang), (1, LANES // HEAD_DIM))
    sin = jnp.tile(jnp.sin(ang), (1, LANES // HEAD_DIM))
    upper = (jnp.arange(LANES) % 32) < 16
    sin_up = jnp.where(upper[None, :], -sin, 0.0)
    sin_dn = jnp.where(upper[None, :], 0.0, sin)
    pad = jnp.zeros((CTX_LEN, LANES), F32)
    return (jnp.concatenate([cos, jnp.ones((CTX_LEN, LANES), F32)], axis=0),
            jnp.concatenate([sin_up, pad], axis=0), jnp.concatenate([sin_dn, pad], axis=0))


def _permute_heads(w, axis):
    heads = [lax.slice_in_dim(w, h * HEAD_DIM, (h + 1) * HEAD_DIM, axis=axis) for h in HEAD_PERM]
    return jnp.concatenate(heads, axis=axis)


def _layout_w_in(w):
    sizes = (GLA_QK, GLA_QK, GLA_WIDTH, GLA_WIDTH, 2 * GLA_GATE_RANK,
             Q_WIDTH, KV_WIDTH, KV_WIDTH, Q_WIDTH, KV_WIDTH, KV_WIDTH)
    w = w.astype(BF16)
    parts, start = [], 0
    for size in sizes:
        parts.append(w[:, start:start + size])
        start += size
    a_q, a_k, a_v, a_r, a_z, b_q, b_k, b_v, c_q, c_k, c_v = parts
    z_pad = jnp.zeros((w.shape[0], LANES - 2 * GLA_GATE_RANK), w.dtype)
    return jnp.concatenate(
        [a_q, a_k, a_v, a_r, _permute_heads(b_q, 1), _permute_heads(c_q, 1), b_k, b_v, c_k, c_v,
         a_z, z_pad], axis=1)


def _layout_w_out(w):
    w = w.astype(BF16)
    a = w[:GLA_WIDTH]
    b = _permute_heads(w[GLA_WIDTH:GLA_WIDTH + Q_WIDTH], 0)
    c = _permute_heads(w[GLA_WIDTH + Q_WIDTH:], 0)
    return jnp.concatenate([a, b, c], axis=0)


def _layout_gate(w_gate):
    out = jnp.zeros((2, LANES, GLA_QK), w_gate.dtype)
    for d in range(2):
        out = out.at[d, d * GLA_GATE_RANK:(d + 1) * GLA_GATE_RANK].set(w_gate[d])
    return out.astype(BF16)


def _diag_reduce_matrix():
    r = jnp.arange(GLA_SUB * LANES)
    c = jnp.arange(GLA_HEADS * GLA_SUB)
    same_s = (r[:, None] // LANES) == (c[None, :] % GLA_SUB)
    same_h = ((r[:, None] % LANES) // GLA_DK) == (c[None, :] // GLA_SUB)
    return jnp.where(same_s & same_h, 1.0, 0.0).astype(BF16)


def kernel(x, c, ctx, c_ctx, w_mod, b_mod, attn_pre_norm, attn_post_norm, ffn_pre_norm,
           ffn_post_norm, w_in, gla_w_gate, gla_b_gate, gla_out_norm, swa_sink, gqa_q_norm,
           gqa_k_norm, w_out, ffn_w_up, ffn_conv_w, ffn_conv_b, ffn_w_down):
    assert x.shape == (BATCH, SEQ, D_MODEL) and ctx.shape == (BATCH, CTX_LEN, D_MODEL)
    cos, sin_up, sin_dn = _rope_tables()
    wst = _diag_reduce_matrix()
    tri = _cumsum_matrices()
    cs = jnp.concatenate([c, c_ctx[None, :], jnp.zeros((16 - BATCH - 1, D_MODEL), F32)], axis=0)
    mod = _modulation(cs, w_mod, b_mod).reshape(DEPTH, 16, MOD_CHUNKS, D_MODEL)

    lat_proj, ctx_seg = _segment(False, PROJ_TILE), _segment(True, CTX_LEN)
    lat_ffn = _segment(False, FFN_TILE)
    lat_att = _segment(False, ATT_TQ)
    x_lat, x_ctx = x, ctx
    for layer in range(DEPTH):
        with_ctx = layer < DEPTH - 1
        row = lambda v: v[layer][None, :]
        tile2 = lambda v: jnp.tile(v[layer], LANES // HEAD_DIM)[None, :]
        proj_args = (mod, layer, row(attn_pre_norm), _layout_w_in(w_in[layer]), cos, sin_up, sin_dn,
                     tile2(gqa_q_norm), tile2(gqa_k_norm))
        proj = _in_proj(x_lat, lat_proj, None, *proj_args)
        gq, gk, gv, gr, gz, bq, bk, bv, cq, ck, cv = _in_proj(x_ctx, ctx_seg, proj, *proj_args)
        o_f, o_b = _gla(gq, gk, gv, gz, _layout_gate(gla_w_gate[layer]),
                        gla_b_gate[layer][:, None, :], tri, wst)
        b_att = _swa(swa_sink[layer].reshape(KV_HEADS, GROUP), bq, bk, bv, with_ctx)
        c_att = _gqa(cq, ck, cv, lat_att, TOK if with_ctx else SEQ, None)
        if with_ctx:
            c_att = _gqa(cq, ck, cv, ctx_seg, TOK, c_att)
        out_args = (o_f, o_b, gr, b_att, c_att, mod, layer, _layout_w_out(w_out[layer]),
                    tile2(gla_out_norm), row(attn_post_norm), row(ffn_pre_norm))
        ffn_args = (mod, layer, ffn_w_up[layer].astype(BF16), ffn_conv_w[layer], row(ffn_conv_b),
                    ffn_w_down[layer].astype(BF16), row(ffn_post_norm))
        x1, h2 = _out_proj(x_lat, lat_proj, *out_args)
        x_lat = _ffn(h2, x1, lat_ffn, *ffn_args)
        if with_ctx:
            x1, h2 = _out_proj(x_ctx, ctx_seg, *out_args)
            x_ctx = _ffn(h2, x1, ctx_seg, *ffn_args)
    return x_lat
```

```python
import functools
import itertools
from typing import NamedTuple

import jax
import jax.numpy as jnp
from jax import lax
from jax.experimental import pallas as pl
from jax.experimental.pallas import tpu as pltpu

F32 = jnp.float32
BF16 = jnp.bfloat16

D_MODEL = 1024
BATCH = 8
SEQ = 2048
DEPTH = 2
CTX_LEN = 256
TOK = SEQ + CTX_LEN
GRID_W = 64
HEAD_DIM = 64
ROPE_THETA = 10000.0
NORM_EPS = 1e-6
MOD_CHUNKS = 6

GLA_HEADS = 4
GLA_DK = 32
GLA_DV = 64
GLA_GATE_RANK = 16
GLA_GATE_TAU = 16.0
GLA_QK = GLA_HEADS * GLA_DK
GLA_WIDTH = GLA_HEADS * GLA_DV

Q_HEADS = 6
KV_HEADS = 2
GROUP = Q_HEADS // KV_HEADS
Q_WIDTH = Q_HEADS * HEAD_DIM
KV_WIDTH = KV_HEADS * HEAD_DIM
SWA_BLOCK = 128

FFN_DIM = 2816
FFN_CHUNK = 256

LANES = 128
PROJ_TILE = 1024
ROW_PART = 256
FFN_TILE = 1024
FFN_PART = 256
HALO = 16
GLA_BLOCK = 256
GLA_CHUNK = 64
GLA_SUB = 16
GLA_BATCH = 4
ATT_TQ = 1024
ATT_ROWS = 512
VMEM_LIMIT = 56 * 1024 * 1024

C_GQ, C_GK, C_GV, C_GR = 0, 128, 256, 512
C_BQ, C_CQ = 768, 1152
C_BK, C_BV, C_CK, C_CV = 1536, 1664, 1792, 1920
C_GZ = 2048
IN_COLS = 2176
HEAD_PERM = (0, 3, 1, 4, 2, 5)

_NT = (((1,), (1,)), ((), ()))


class _Segment(NamedTuple):
    tile: int
    n_tiles: int
    first: int
    ctx: bool


def _segment(ctx, tile):
    if ctx:
        return _Segment(CTX_LEN, 1, SEQ // CTX_LEN, True)
    return _Segment(tile, SEQ // tile, 0, False)


def _dot(a, b):
    return jnp.dot(a, b, preferred_element_type=F32)


def _dot_nt(a, b):
    return lax.dot_general(a, b, _NT, preferred_element_type=F32)


def _idiv(x, n):
    assert n & (n - 1) == 0
    return lax.shift_right_logical(x, jnp.int32(n.bit_length() - 1))


def _split2(x):
    hi = x.astype(BF16)
    lo = (x - hi.astype(F32)).astype(BF16)
    return hi, lo


def _split3(x):
    h1 = x.astype(BF16)
    r1 = x - h1.astype(F32)
    h2 = r1.astype(BF16)
    h3 = (r1 - h2.astype(F32)).astype(BF16)
    return h1, h2, h3


def _seg_ones(width, seg):
    r = _idiv(lax.broadcasted_iota(jnp.int32, (width, width), 0), seg)
    c = _idiv(lax.broadcasted_iota(jnp.int32, (width, width), 1), seg)
    return jnp.where(r == c, 1.0, 0.0).astype(BF16)


def _seg_sum(x, ones):
    hi, lo = _split2(x)
    return _dot(hi, ones) + _dot(lo, ones)


def _row_rms(x, w):
    ms = jnp.mean(x * x, axis=-1, keepdims=True)
    return x * lax.rsqrt(ms + NORM_EPS) * w


def _head_rms(x, ones, w):
    ss = _seg_sum(x * x, ones)
    return x * lax.rsqrt(ss * (1.0 / HEAD_DIM) + NORM_EPS) * w


def _silu(x):
    return x / (1.0 + jnp.exp(-x))


def _rope(x, cos, sin_up, sin_dn):
    return x * cos + pltpu.roll(x, LANES - 16, 1) * sin_up + pltpu.roll(x, 16, 1) * sin_dn


def _const_spec(shape):
    return pl.BlockSpec(shape, lambda b, i: (0,) * len(shape))


def _params(*semantics):
    return pltpu.CompilerParams(dimension_semantics=semantics, vmem_limit_bytes=VMEM_LIMIT)


def _mod_kernel(c_ref, w_ref, b_ref, o_ref):
    a = _silu(c_ref[...]).astype(BF16)
    o_ref[0] = _dot(a, w_ref[0].astype(BF16)) + b_ref[0]


def _modulation(cs, w_mod, b_mod):
    tn = 1536
    n = MOD_CHUNKS * D_MODEL
    return pl.pallas_call(
        _mod_kernel,
        grid=(DEPTH, n // tn),
        in_specs=[
            pl.BlockSpec((16, D_MODEL), lambda l, j: (0, 0)),
            pl.BlockSpec((1, D_MODEL, tn), lambda l, j: (l, 0, j)),
            pl.BlockSpec((1, 1, tn), lambda l, j: (l, 0, j)),
        ],
        out_specs=pl.BlockSpec((1, 16, tn), lambda l, j: (l, 0, j)),
        out_shape=jax.ShapeDtypeStruct((DEPTH, 16, n), F32),
        compiler_params=_params("arbitrary", "arbitrary"),
        name="modulation",
    )(cs, w_mod, b_mod.reshape(DEPTH, 1, n))


def _mod_spec(layer, seg):
    return pl.BlockSpec((1, 1, MOD_CHUNKS, D_MODEL),
                        lambda b, i: (layer, BATCH if seg.ctx else b, 0, 0))


_PROJ_OUTPUTS = ((GLA_QK, F32), (GLA_QK, F32), (GLA_WIDTH, F32), (GLA_WIDTH, F32), (LANES, F32),
                 (Q_WIDTH, BF16), (KV_WIDTH, BF16), (KV_WIDTH, BF16),
                 (Q_WIDTH, BF16), (KV_WIDTH, BF16), (KV_WIDTH, BF16))


def _interleave(streams):
    for _ in itertools.zip_longest(*streams):
        pass


def _lag(stream, stages):
    for _ in range(stages):
        yield
    yield from stream


def _in_proj_kernel(x_ref, mod_ref, nw_ref, w_ref, cos_ref, su_ref, sd_ref, qn_ref, kn_ref, *rest):
    (gq_ref, gk_ref, gv_ref, gr_ref, gz_ref, bq_ref, bk_ref, bv_ref,
     cq_ref, ck_ref, cv_ref) = rest[-len(_PROJ_OUTPUTS):]
    sh = mod_ref[0, 0, 0:1, :]
    sc = mod_ref[0, 0, 1:2, :]
    ones = _seg_ones(LANES, HEAD_DIM)
    att_scale = HEAD_DIM ** -0.5

    def part(rows):
        h = (_row_rms(x_ref[0, rows, :], nw_ref[...]) * (1.0 + sc) + sh).astype(BF16)

        def proj(c0, width):
            return _dot(h, w_ref[:, c0:c0 + width])

        cos, su, sd = cos_ref[rows, :], su_ref[rows, :], sd_ref[rows, :]
        p_a = proj(C_GQ, C_BQ - C_GQ)
        p_q = proj(C_BQ, C_BK - C_BQ)
        yield
        gq_ref[0, rows, :] = p_a[:, C_GQ:C_GQ + GLA_QK] * (GLA_DK ** -0.5)
        gk_ref[0, rows, :] = p_a[:, C_GK:C_GK + GLA_QK]
        gv_ref[0, rows, :] = p_a[:, C_GV:C_GV + GLA_WIDTH]
        gr_ref[0, rows, :] = p_a[:, C_GR:C_GR + GLA_WIDTH]
        p_kv = proj(C_BK, IN_COLS - C_BK)

        def kv(c0):
            return p_kv[:, c0 - C_BK:c0 - C_BK + KV_WIDTH]
        yield
        for g in range(GROUP):
            ls = slice(g * LANES, (g + 1) * LANES)
            bq_ref[0, rows, ls] = (_rope(p_q[:, ls], cos, su, sd) * att_scale).astype(BF16)
            cq = _head_rms(p_q[:, Q_WIDTH + g * LANES:Q_WIDTH + (g + 1) * LANES], ones, qn_ref[...])
            cq_ref[0, rows, ls] = (_rope(cq, cos, su, sd) * att_scale).astype(BF16)
        yield
        bk_ref[0, rows, :] = _rope(kv(C_BK), cos, su, sd).astype(BF16)
        bv_ref[0, rows, :] = kv(C_BV).astype(BF16)
        ck_ref[0, rows, :] = _rope(_head_rms(kv(C_CK), ones, kn_ref[...]), cos, su, sd).astype(BF16)
        cv_ref[0, rows, :] = kv(C_CV).astype(BF16)
        gz_ref[0, rows, :] = kv(C_GZ)
        yield

    tile = x_ref.shape[1]
    _interleave([part(slice(r, r + ROW_PART)) for r in range(0, tile, ROW_PART)])


def _in_proj(x_seg, seg, into, mod, layer, nw, w, cos, su, sd, qn, kn):
    tile, first = seg.tile, seg.first

    def rows(width):
        return pl.BlockSpec((1, tile, width), lambda b, i: (b, first + i, 0))

    def table():
        return pl.BlockSpec((tile, LANES), lambda b, i: (first + i, 0))

    in_specs = [pl.BlockSpec((1, tile, D_MODEL), lambda b, i: (b, i, 0)), _mod_spec(layer, seg),
                _const_spec((1, D_MODEL)), _const_spec((D_MODEL, IN_COLS)),
                table(), table(), table(), _const_spec((1, LANES)), _const_spec((1, LANES))]
    args = [x_seg, mod, nw, w, cos, su, sd, qn, kn]
    aliases = {}
    if into is not None:
        aliases = {len(args) + k: k for k in range(len(into))}
        in_specs += [pl.BlockSpec(memory_space=pl.ANY)] * len(into)
        args += list(into)
    return pl.pallas_call(
        _in_proj_kernel,
        grid=(BATCH, seg.n_tiles),
        in_specs=in_specs,
        out_specs=[rows(wd) for wd, _ in _PROJ_OUTPUTS],
        out_shape=[jax.ShapeDtypeStruct((BATCH, TOK, wd), dt) for wd, dt in _PROJ_OUTPUTS],
        input_output_aliases=aliases,
        compiler_params=_params("parallel", "parallel"),
        name="in_proj_ctx" if seg.ctx else "in_proj",
    )(*args)


def _lane_group_mask(width, group, h):
    lane = lax.broadcasted_iota(jnp.int32, (1, width), 1)
    return _idiv(lane, group) == h


def _gla_direction(q_ref, k_ref, v_ref, z_ref, wg, bg, tri, wst_ref, o_ref, st_ref, b_s, e_s, rev):
    nb, c_len, sb = GLA_BLOCK, GLA_CHUNK, GLA_SUB
    n_chunks, n_sub = nb // c_len, c_len // sb

    g = _dot(z_ref[0].astype(BF16), wg) + bg
    la = (jnp.minimum(g, 0.0) - jnp.log1p(jnp.exp(-jnp.abs(g)))) * (1.0 / GLA_GATE_TAU)

    h1, h2, h3 = _split3(la)
    b = _dot(tri, h1) + _dot(tri, h2) + _dot(tri, h3)
    b_s[...] = b
    yield

    q, k, v = q_ref[0], k_ref[0], v_ref[0]

    half = sb // 2
    t_half = lax.broadcasted_iota(jnp.int32, (half, 1), 0)

    for jb in range(nb // sb):
        r0 = jb * sb
        bb = [b_s[pl.ds(r0 + lo, half), :] for lo in (0, half)]
        qb = [q_ref[0, pl.ds(r0 + lo, half), :] for lo in (0, half)]
        for s in range(sb):
            bs = b_s[pl.ds(r0 + s, 1), :]
            ks = k_ref[0, pl.ds(r0 + s, 1), :]
            halves = []
            for hi, lo in enumerate((0, half)):
                none = (lo > s) if rev else (lo + half - 1 < s)
                every = (lo + half - 1 <= s) if rev else (lo >= s)
                if none:
                    halves.append(jnp.zeros((half, LANES), F32))
                    continue
                diff = bb[hi] - bs
                if every:
                    e = qb[hi] * ks * jnp.exp(diff)
                else:
                    keep = (t_half + lo <= s) if rev else (t_half + lo >= s)
                    e = jnp.where(keep, qb[hi] * ks * jnp.exp(jnp.minimum(diff, 0.0)), 0.0)
                halves.append(e)
            e_s[pl.ds(r0, sb), s * LANES:(s + 1) * LANES] = jnp.concatenate(
                halves, axis=0).astype(BF16)
        if jb % 4 == 3:
            yield

    a_diag = _dot(e_s[...], wst_ref[...])
    yield

    hm_k = [_lane_group_mask(GLA_QK, GLA_DK, h) for h in range(GLA_HEADS)]
    hm_v = [_lane_group_mask(GLA_WIDTH, GLA_DV, h) for h in range(GLA_HEADS)]
    st_row = _idiv(lax.broadcasted_iota(jnp.int32, (GLA_WIDTH, GLA_QK), 0), GLA_DV)
    st_col = _idiv(lax.broadcasted_iota(jnp.int32, (GLA_WIDTH, GLA_QK), 1), GLA_DK)
    st_mask = st_row == st_col
    v_t = jnp.transpose(v).astype(BF16)
    chunk = [slice(c * c_len, (c + 1) * c_len) for c in range(n_chunks)]
    sub = [slice(j * sb, (j + 1) * sb) for j in range(n_sub)]

    q_hat, kv_t, d_row = [], [], []
    for c in range(n_chunks):
        bc = b[chunk[c]]
        end = 0 if rev else c_len - 1
        b_end = bc[end:end + 1]
        q_hat.append((q[chunk[c]] * jnp.exp(bc)).astype(BF16))
        d_row.append(jnp.exp(b_end))
        k_hat = (k[chunk[c]] * jnp.exp(b_end - bc)).astype(BF16)
        pads = [jnp.zeros((n, GLA_QK), BF16) for n in (c * c_len, nb - (c + 1) * c_len) if n]
        k_hat = jnp.concatenate(pads[:1 if c else 0] + [k_hat] + pads[1 if c else 0:], axis=0)
        kv_t.append(jnp.where(st_mask, _dot(v_t, k_hat), 0.0))
    yield

    scores = {}
    for c in range(n_chunks):
        bc, qc, kc = b[chunk[c]], q[chunk[c]], k[chunk[c]]
        for j in range(n_sub):
            if not ((j > 0) if rev else (j < n_sub - 1)):
                continue
            edge = j * sb if rev else (j + 1) * sb - 1
            ref_b = bc[edge:edge + 1]
            later = slice(0, j * sb) if rev else slice((j + 1) * sb, c_len)
            q_l = (qc[later] * jnp.exp(bc[later] - ref_b)).astype(BF16)
            k_j = kc[sub[j]] * jnp.exp(ref_b - bc[sub[j]])
            k_tile = jnp.concatenate(
                [jnp.where(hm_k[h], k_j, 0.0) for h in range(GLA_HEADS)], axis=0).astype(BF16)
            scores[(c, j)] = _dot_nt(q_l, k_tile)
        yield

    o_intra = []
    for c in range(n_chunks):
        vc = v[chunk[c]]
        blocks = [None] * n_sub
        for j in range(n_sub):
            v_exp = jnp.concatenate(
                [jnp.where(hm_v[h], vc[sub[j]], 0.0) for h in range(GLA_HEADS)],
                axis=0).astype(BF16)
            ad = a_diag[c * c_len + j * sb:c * c_len + (j + 1) * sb]
            if (c, j) in scores:
                sc = scores[(c, j)]
                p = jnp.concatenate([sc, ad], axis=0) if rev else jnp.concatenate([ad, sc], axis=0)
                first = 0 if rev else j
            else:
                p, first = ad, j
            contrib = _dot(p.astype(BF16), v_exp)
            for r in range(contrib.shape[0] // sb):
                piece = contrib[r * sb:(r + 1) * sb]
                blocks[first + r] = piece if blocks[first + r] is None else blocks[first + r] + piece
        o_intra.append(jnp.concatenate(blocks, axis=0))
        yield

    state = st_ref[...]
    outs = [None] * n_chunks
    for c in (reversed(range(n_chunks)) if rev else range(n_chunks)):
        outs[c] = o_intra[c] + _dot_nt(q_hat[c], state.astype(BF16))
        state = state * d_row[c] + kv_t[c]
        yield
    st_ref[...] = state
    o_ref[0, 0] = jnp.concatenate(outs, axis=0)


def _gla_kernel(qf, kf, vf, zf, qb, kb, vb, zb, wg_ref, bg_ref, tri_ref, wst_ref, of_ref, ob_ref,
                sf_ref, sb_ref, bf_s, bb_s, ef_s, eb_s):
    @pl.when(pl.program_id(1) == 0)
    def _():
        sf_ref[...] = jnp.zeros_like(sf_ref)
        sb_ref[...] = jnp.zeros_like(sb_ref)

    streams = []
    for bi in range(GLA_BATCH):
        one = pl.ds(bi, 1)
        streams.append(_gla_direction(
            qf.at[one], kf.at[one], vf.at[one], zf.at[one], wg_ref[0], bg_ref[0], tri_ref[0],
            wst_ref, of_ref.at[:, one], sf_ref.at[bi], bf_s.at[bi], ef_s.at[bi], False))
        streams.append(_gla_direction(
            qb.at[one], kb.at[one], vb.at[one], zb.at[one], wg_ref[1], bg_ref[1], tri_ref[1],
            wst_ref, ob_ref.at[:, one], sb_ref.at[bi], bb_s.at[bi], eb_s.at[bi], True))
    _interleave([_lag(s, n) for n, s in enumerate(streams)])


def _cumsum_matrices():
    t = jnp.arange(GLA_BLOCK)[:, None]
    s = jnp.arange(GLA_BLOCK)[None, :]
    inside = (t // GLA_CHUNK) == (s // GLA_CHUNK)
    return jnp.stack([jnp.where(inside & (s <= t), 1.0, 0.0),
                      jnp.where(inside & (s >= t), 1.0, 0.0)]).astype(BF16)


def _gla(gq, gk, gv, gz, wg, bg, tri, wst):
    n_lat = SEQ // GLA_BLOCK
    ctx_blk = n_lat

    def fwd(b, s):
        return (b, jnp.where(s == 0, ctx_blk, s - 1), 0)

    def bwd(b, s):
        return (b, jnp.where(s == 0, ctx_blk, n_lat - s), 0)

    def specs(index_map):
        return [pl.BlockSpec((GLA_BATCH, GLA_BLOCK, wd), index_map)
                for wd in (GLA_QK, GLA_QK, GLA_WIDTH, LANES)]

    return pl.pallas_call(
        _gla_kernel,
        grid=(BATCH // GLA_BATCH, n_lat + 1),
        in_specs=specs(fwd) + specs(bwd) + [
            _const_spec((2, LANES, GLA_QK)), _const_spec((2, 1, GLA_QK)),
            _const_spec((2, GLA_BLOCK, GLA_BLOCK)),
            _const_spec((GLA_SUB * LANES, GLA_HEADS * GLA_SUB))],
        out_specs=[pl.BlockSpec((1, GLA_BATCH, GLA_BLOCK, GLA_WIDTH), lambda b, s: (0,) + fwd(b, s)),
                   pl.BlockSpec((1, GLA_BATCH, GLA_BLOCK, GLA_WIDTH), lambda b, s: (0,) + bwd(b, s))],
        out_shape=[jax.ShapeDtypeStruct((1, BATCH, TOK, GLA_WIDTH), F32)] * 2,
        scratch_shapes=[pltpu.VMEM((GLA_BATCH, GLA_WIDTH, GLA_QK), F32)] * 2
        + [pltpu.VMEM((GLA_BATCH, GLA_BLOCK, GLA_QK), F32)] * 2
        + [pltpu.VMEM((GLA_BATCH, GLA_BLOCK, GLA_SUB * LANES), BF16)] * 2,
        compiler_params=_params("parallel", "arbitrary"),
        name="gla",
    )(gq, gk, gv, gz, gq, gk, gv, gz, wg, bg, tri, wst)


def _fold_lanes(x, op):
    acc = x[:, 0:LANES]
    for c in range(1, x.shape[1] // LANES):
        acc = op(acc, x[:, c * LANES:(c + 1) * LANES])
    return acc


class _Job(NamedTuple):
    rows: slice
    keys: jax.Array
    values: jax.Array
    bias: jax.Array


def _attend(q_ref, jobs, sink_ref, groups, ahead):
    lane = lax.broadcasted_iota(jnp.int32, (1, LANES), 1)
    first = lane < HEAD_DIM
    units = [(jb, h, gs) for jb in range(len(jobs)) for gs in groups for h in range(KV_HEADS)]

    def scores(jb, h, gs):
        mine = first if h == 0 else jnp.logical_not(first)
        qz = [jnp.where(mine, q_ref[0, jobs[jb].rows, g * LANES:(g + 1) * LANES], 0) for g in gs]
        qz = qz[0] if len(gs) == 1 else jnp.concatenate(qz, axis=0)
        return _dot_nt(qz, jobs[jb].keys)

    pending = {}
    for idx in range(min(ahead, len(units))):
        pending[idx] = scores(*units[idx])
    outs = {}
    for idx, (jb, h, gs) in enumerate(units):
        if idx + ahead < len(units):
            pending[idx + ahead] = scores(*units[idx + ahead])
        s = pending.pop(idx)
        job = jobs[jb]
        tq = job.rows.stop - job.rows.start
        if job.bias is not None:
            nb = job.bias.shape[1]
            s = jnp.concatenate([s[:, :nb] + job.bias[:s.shape[0]], s[:, nb:]], axis=1)
        m = jnp.max(_fold_lanes(s, jnp.maximum), axis=-1, keepdims=True)
        if sink_ref is not None:
            sk = [jnp.full((tq, 1), sink_ref[h, g], F32) for g in gs]
            sk = sk[0] if len(gs) == 1 else jnp.concatenate(sk, axis=0)
            m = jnp.maximum(m, sk)
        p = jnp.exp(s - m)
        den = jnp.sum(_fold_lanes(p, jnp.add), axis=-1, keepdims=True)
        if sink_ref is not None:
            den = den + jnp.exp(sk - m)
        o = _dot(p.astype(BF16), job.values) / den
        for r, g in enumerate(gs):
            outs[(jb, h, g)] = o[r * tq:(r + 1) * tq]
    return [jnp.concatenate([jnp.where(first, outs[(jb, 0, g)], outs[(jb, 1, g)])
                             for g in range(GROUP)], axis=1) for jb in range(len(jobs))]


_ALL_GROUPS = (tuple(range(GROUP)),)
_EACH_GROUP = tuple((g,) for g in range(GROUP))


def _swa_kernel(sink_ref, band_ref, q_ref, kc_ref, vc_ref, kp_ref, kk_ref, kn_ref, vp_ref, vk_ref,
                vn_ref, o_ref, *, n_lat):
    i = pl.program_id(1)
    w = SWA_BLOCK

    @pl.when(i >= n_lat)
    def _():
        job = _Job(slice(0, 2 * w), kc_ref[0], vc_ref[0], None)
        o_ref[0] = _attend(q_ref, [job], sink_ref, _ALL_GROUPS, 2)[0].astype(o_ref.dtype)

    @pl.when(i < n_lat)
    def _():
        k_lo, k_hi = kk_ref[0, 0:w, :], kk_ref[0, w:2 * w, :]
        v_lo, v_hi = vk_ref[0, 0:w, :], vk_ref[0, w:2 * w, :]
        col = lax.broadcasted_iota(jnp.int32, (1, 3 * w), 1)
        no_prev = jnp.where((col < w) & (i == 0), -1e30, 0.0)
        no_next = jnp.where((col >= 2 * w) & (i == n_lat - 1), -1e30, 0.0)
        jobs = [
            _Job(slice(0, w), jnp.concatenate([kp_ref[0], k_lo, k_hi, kc_ref[0]], axis=0),
                 jnp.concatenate([vp_ref[0], v_lo, v_hi, vc_ref[0]], axis=0),
                 band_ref[...] + no_prev),
            _Job(slice(w, 2 * w), jnp.concatenate([k_lo, k_hi, kn_ref[0], kc_ref[0]], axis=0),
                 jnp.concatenate([v_lo, v_hi, vn_ref[0], vc_ref[0]], axis=0),
                 band_ref[...] + no_next)]
        outs = _attend(q_ref, jobs, sink_ref, _ALL_GROUPS, 4)
        o_ref[0] = jnp.concatenate(outs, axis=0).astype(o_ref.dtype)


def _swa(sink, bq, bk, bv, with_ctx):
    w = SWA_BLOCK
    n_blocks = SEQ // w
    n_lat = n_blocks // 2
    n_steps = (TOK if with_ctx else SEQ) // (2 * w)

    def edge_block(which):
        return pl.BlockSpec(
            (1, w, KV_WIDTH),
            lambda b, i: (b, jnp.clip(2 * i - 1 if which < 0 else 2 * i + 2, 0, n_blocks - 1), 0))

    own_spec = pl.BlockSpec((1, 2 * w, KV_WIDTH), lambda b, i: (b, jnp.minimum(i, n_lat - 1), 0))

    t = jnp.arange(GROUP * w)[:, None] % w
    kk = jnp.arange(3 * w)[None, :]
    band = jnp.where((kk - t >= 0) & (kk - t <= 2 * w), 0.0, -1e30).astype(F32)

    ctx_spec = pl.BlockSpec((1, CTX_LEN, KV_WIDTH), lambda b, i: (b, SEQ // CTX_LEN, 0))
    return pl.pallas_call(
        functools.partial(_swa_kernel, n_lat=n_lat),
        grid=(BATCH, n_steps),
        in_specs=[pl.BlockSpec(memory_space=pltpu.SMEM),
                  _const_spec((GROUP * w, 3 * w)),
                  pl.BlockSpec((1, 2 * w, Q_WIDTH), lambda b, i: (b, i, 0)),
                  ctx_spec, ctx_spec,
                  edge_block(-1), own_spec, edge_block(1),
                  edge_block(-1), own_spec, edge_block(1)],
        out_specs=pl.BlockSpec((1, 2 * w, Q_WIDTH), lambda b, i: (b, i, 0)),
        out_shape=jax.ShapeDtypeStruct((BATCH, n_steps * 2 * w, Q_WIDTH), BF16),
        compiler_params=_params("parallel", "parallel"),
        name="swa",
    )(sink, band, bq, bk, bv, bk, bk, bk, bv, bv, bv)


def _gqa_kernel(q_ref, k_ref, v_ref, *rest, ctx):
    o_ref = rest[-1]
    if ctx:
        keys, values = k_ref[0, SEQ:TOK, :], v_ref[0, SEQ:TOK, :]
    else:
        keys, values = k_ref[0], v_ref[0]
    tq = q_ref.shape[1]
    rows = min(tq, ATT_ROWS)
    jobs = [_Job(slice(r, r + rows), keys, values, None) for r in range(0, tq, rows)]
    outs = _attend(q_ref, jobs, None, _EACH_GROUP, 1)
    o_ref[0] = (outs[0] if len(outs) == 1 else jnp.concatenate(outs, axis=0)).astype(o_ref.dtype)


def _gqa(cq, ck, cv, seg, rows, into):
    tile, first = seg.tile, seg.first
    kv_spec = pl.BlockSpec((1, TOK, KV_WIDTH), lambda b, i: (b, 0, 0))
    q_spec = pl.BlockSpec((1, tile, Q_WIDTH), lambda b, i: (b, first + i, 0))
    in_specs, args, aliases = [q_spec, kv_spec, kv_spec], [cq, ck, cv], {}
    if into is not None:
        in_specs.append(pl.BlockSpec(memory_space=pl.ANY))
        args.append(into)
        aliases = {3: 0}
    return pl.pallas_call(
        functools.partial(_gqa_kernel, ctx=seg.ctx),
        grid=(BATCH, seg.n_tiles),
        in_specs=in_specs,
        out_specs=q_spec,
        out_shape=jax.ShapeDtypeStruct((BATCH, rows, Q_WIDTH), BF16),
        input_output_aliases=aliases,
        compiler_params=_params("parallel", "parallel"),
        name="gqa_ctx" if seg.ctx else "gqa",
    )(*args)


def _out_proj_kernel(x_ref, of_ref, ob_ref, r_ref, b_ref, c_ref, mod_ref, w_ref, on_ref,
                     post_ref, pre_ref, x1_ref, h2_ref):
    ones = _seg_ones(LANES, GLA_DV)
    g1 = mod_ref[0, 0, 2:3, :]
    sh2 = mod_ref[0, 0, 3:4, :]
    sc2 = mod_ref[0, 0, 4:5, :]

    def part(rows):
        mix = _dot(b_ref[0, rows, :], w_ref[GLA_WIDTH:GLA_WIDTH + Q_WIDTH, :])
        mix = mix + _dot(c_ref[0, rows, :], w_ref[GLA_WIDTH + Q_WIDTH:, :])
        ya = []
        for lp in range(GLA_WIDTH // LANES):
            ls = slice(lp * LANES, (lp + 1) * LANES)
            o = of_ref[0, 0, rows, ls] + ob_ref[0, 0, rows, ls]
            ya.append((_head_rms(o, ones, on_ref[...]) * _silu(r_ref[0, rows, ls])).astype(BF16))
        mix = mix + _dot(jnp.concatenate(ya, axis=1), w_ref[0:GLA_WIDTH, :])
        yield
        x1 = x_ref[0, rows, :] + g1 * _row_rms(mix, post_ref[...])
        x1_ref[0, rows, :] = x1
        h2_ref[0, rows, :] = (_row_rms(x1, pre_ref[...]) * (1.0 + sc2) + sh2).astype(BF16)
        yield

    tile = x_ref.shape[1]
    _interleave([part(slice(r, r + ROW_PART)) for r in range(0, tile, ROW_PART)])


def _out_proj(x_seg, seg, o_f, o_b, gr, b_att, c_att, mod, layer, w, on, post, pre):
    tile, first = seg.tile, seg.first

    def own(width):
        return pl.BlockSpec((1, tile, width), lambda b, i: (b, i, 0))

    def combined(width):
        return pl.BlockSpec((1, tile, width), lambda b, i: (b, first + i, 0))

    gla_spec = pl.BlockSpec((1, 1, tile, GLA_WIDTH), lambda b, i: (0, b, first + i, 0))
    rows = seg.n_tiles * tile
    return pl.pallas_call(
        _out_proj_kernel,
        grid=(BATCH, seg.n_tiles),
        in_specs=[own(D_MODEL), gla_spec, gla_spec, combined(GLA_WIDTH),
                  combined(Q_WIDTH), combined(Q_WIDTH), _mod_spec(layer, seg),
                  _const_spec((D_MODEL, D_MODEL)), _const_spec((1, LANES)),
                  _const_spec((1, D_MODEL)), _const_spec((1, D_MODEL))],
        out_specs=[own(D_MODEL), own(D_MODEL)],
        out_shape=[jax.ShapeDtypeStruct((BATCH, rows, D_MODEL), F32),
                   jax.ShapeDtypeStruct((BATCH, rows, D_MODEL), BF16)],
        compiler_params=_params("parallel", "parallel"),
        name="out_proj_ctx" if seg.ctx else "out_proj",
    )(x_seg, o_f, o_b, gr, b_att, c_att, mod, w, on, post, pre)


def _ffn_kernel(hp_ref, h_ref, hn_ref, x1_ref, mod_ref, wu_ref, cw_ref, cb_ref, wd_ref, post_ref,
                o_ref, *, n_tiles):
    i = pl.program_id(1)
    tile = h_ref.shape[1]
    hp = jnp.where(i != 0, hp_ref[0], 0)
    hn = jnp.where(i != n_tiles - 1, hn_ref[0], 0)
    h_ext = jnp.concatenate([hp, h_ref[0], hn], axis=0)
    part = min(tile, FFN_PART)
    win = part + 2 * HALO
    mid = slice(HALO, HALO + part)

    def conv(u, c0):
        cw = cw_ref[:, c0:c0 + FFN_CHUNK]
        prev = pltpu.roll(u, 1, 0)[mid]
        nxt = pltpu.roll(u, win - 1, 0)[mid]
        return cw[0:1] * prev + cw[1:2] * u[mid] + cw[2:3] * nxt + cb_ref[:, c0:c0 + FFN_CHUNK]

    n_fc = FFN_DIM // FFN_CHUNK

    n_parts = tile // part
    cuts = [0] + [HALO + r * part for r in range(1, n_parts)] + [tile + 2 * HALO]
    lhs = [h_ext[cuts[r]:cuts[r + 1]] for r in range(n_parts)]

    def window(us, r):
        lo, hi = r * part, r * part + win
        pieces = []
        for q in range(n_parts):
            a, b = max(lo, cuts[q]), min(hi, cuts[q + 1])
            if a < b:
                pieces.append(us[q][a - cuts[q]:b - cuts[q]])
        return pieces[0] if len(pieces) == 1 else jnp.concatenate(pieces, axis=0)

    def up(fc):
        ca = fc * FFN_CHUNK
        cg = FFN_DIM + ca
        return ([_dot(x, wu_ref[:, ca:ca + FFN_CHUNK]) for x in lhs],
                [_dot(x, wu_ref[:, cg:cg + FFN_CHUNK]) for x in lhs])

    acc = [jnp.zeros((part, D_MODEL), F32) for _ in range(n_parts)]
    ua, ug = up(0)
    for fc in range(n_fc):
        nxt_u = up(fc + 1) if fc + 1 < n_fc else None
        ca = fc * FFN_CHUNK
        for r in range(n_parts):
            act = (_silu(conv(window(ua, r), ca)) * conv(window(ug, r), FFN_DIM + ca)).astype(BF16)
            acc[r] = acc[r] + _dot(act, wd_ref[ca:ca + FFN_CHUNK, :])
        if nxt_u is not None:
            ua, ug = nxt_u
    g2 = mod_ref[0, 0, 5:6, :]
    for r in range(tile // part):
        rows = slice(r * part, (r + 1) * part)
        o_ref[0, rows, :] = x1_ref[0, rows, :] + g2 * _row_rms(acc[r], post_ref[...])


def _ffn(h2, x1, seg, mod, layer, wu, cw, cb, wd, post):
    tile, n_tiles = seg.tile, seg.n_tiles
    per = tile // HALO
    n_halo = n_tiles * per

    def own(width):
        return pl.BlockSpec((1, tile, width), lambda b, i: (b, i, 0))

    prev_spec = pl.BlockSpec((1, HALO, D_MODEL), lambda b, i: (b, jnp.maximum(i * per - 1, 0), 0))
    next_spec = pl.BlockSpec((1, HALO, D_MODEL),
                             lambda b, i: (b, jnp.minimum((i + 1) * per, n_halo - 1), 0))
    return pl.pallas_call(
        functools.partial(_ffn_kernel, n_tiles=n_tiles),
        grid=(BATCH, n_tiles),
        in_specs=[prev_spec, own(D_MODEL), next_spec, own(D_MODEL), _mod_spec(layer, seg),
                  _const_spec((D_MODEL, 2 * FFN_DIM)), _const_spec((3, 2 * FFN_DIM)),
                  _const_spec((1, 2 * FFN_DIM)), _const_spec((FFN_DIM, D_MODEL)),
                  _const_spec((1, D_MODEL))],
        out_specs=own(D_MODEL),
        out_shape=jax.ShapeDtypeStruct((BATCH, n_tiles * tile, D_MODEL), F32),
        compiler_params=pltpu.CompilerParams(
            dimension_semantics=("parallel", "parallel"), vmem_limit_bytes=VMEM_LIMIT,
            ),
        name="ffn_ctx" if seg.ctx else "ffn",
    )(h2, h2, h2, x1, mod, wu, cw, cb, wd, post)


def _rope_tables():
    rows = SEQ // GRID_W
    row = jnp.repeat(jnp.arange(rows), GRID_W).astype(F32)
    col = (jnp.arange(rows * GRID_W) % GRID_W).astype(F32)
    n_freq = HEAD_DIM // 4
    inv_freq = ROPE_THETA ** (-jnp.arange(n_freq, dtype=F32) / n_freq)
    ang_r = row[:, None] * inv_freq[None, :]
    ang_c = col[:, None] * inv_freq[None, :]
    ang = jnp.concatenate([ang_r, ang_r, ang_c, ang_c], axis=-1)
    cos = jnp.tile(jnp.cos(ang), (1, LANES // HEAD_DIM))
    sin = jnp.tile(jnp.sin(ang), (1, LANES // HEAD_DIM))
    upper = (jnp.arange(LANES) % 32) < 16
    sin_up = jnp.where(upper[None, :], -sin, 0.0)
    sin_dn = jnp.where(upper[None, :], 0.0, sin)
    pad = jnp.zeros((CTX_LEN, LANES), F32)
    return (jnp.concatenate([cos, jnp.ones((CTX_LEN, LANES), F32)], axis=0),
            jnp.concatenate([sin_up, pad], axis=0), jnp.concatenate([sin_dn, pad], axis=0))


def _permute_heads(w, axis):
    heads = [lax.slice_in_dim(w, h * HEAD_DIM, (h + 1) * HEAD_DIM, axis=axis) for h in HEAD_PERM]
    return jnp.concatenate(heads, axis=axis)


def _layout_w_in(w):
    sizes = (GLA_QK, GLA_QK, GLA_WIDTH, GLA_WIDTH, 2 * GLA_GATE_RANK,
             Q_WIDTH, KV_WIDTH, KV_WIDTH, Q_WIDTH, KV_WIDTH, KV_WIDTH)
    w = w.astype(BF16)
    parts, start = [], 0
    for size in sizes:
        parts.append(w[:, start:start + size])
        start += size
    a_q, a_k, a_v, a_r, a_z, b_q, b_k, b_v, c_q, c_k, c_v = parts
    z_pad = jnp.zeros((w.shape[0], LANES - 2 * GLA_GATE_RANK), w.dtype)
    return jnp.concatenate(
        [a_q, a_k, a_v, a_r, _permute_heads(b_q, 1), _permute_heads(c_q, 1), b_k, b_v, c_k, c_v,
         a_z, z_pad], axis=1)


def _layout_w_out(w):
    w = w.astype(BF16)
    a = w[:GLA_WIDTH]
    b = _permute_heads(w[GLA_WIDTH:GLA_WIDTH + Q_WIDTH], 0)
    c = _permute_heads(w[GLA_WIDTH + Q_WIDTH:], 0)
    return jnp.concatenate([a, b, c], axis=0)


def _layout_gate(w_gate):
    out = jnp.zeros((2, LANES, GLA_QK), w_gate.dtype)
    for d in range(2):
        out = out.at[d, d * GLA_GATE_RANK:(d + 1) * GLA_GATE_RANK].set(w_gate[d])
    return out.astype(BF16)


def _diag_reduce_matrix():
    r = jnp.arange(GLA_SUB * LANES)
    c = jnp.arange(GLA_HEADS * GLA_SUB)
    same_s = (r[:, None] // LANES) == (c[None, :] % GLA_SUB)
    same_h = ((r[:, None] % LANES) // GLA_DK) == (c[None, :] // GLA_SUB)
    return jnp.where(same_s & same_h, 1.0, 0.0).astype(BF16)


def kernel(x, c, ctx, c_ctx, w_mod, b_mod, attn_pre_norm, attn_post_norm, ffn_pre_norm,
           ffn_post_norm, w_in, gla_w_gate, gla_b_gate, gla_out_norm, swa_sink, gqa_q_norm,
           gqa_k_norm, w_out, ffn_w_up, ffn_conv_w, ffn_conv_b, ffn_w_down):
    assert x.shape == (BATCH, SEQ, D_MODEL) and ctx.shape == (BATCH, CTX_LEN, D_MODEL)
    cos, sin_up, sin_dn = _rope_tables()
    wst = _diag_reduce_matrix()
    tri = _cumsum_matrices()
    cs = jnp.concatenate([c, c_ctx[None, :], jnp.zeros((16 - BATCH - 1, D_MODEL), F32)], axis=0)
    mod = _modulation(cs, w_mod, b_mod).reshape(DEPTH, 16, MOD_CHUNKS, D_MODEL)

    lat_proj, ctx_seg = _segment(False, PROJ_TILE), _segment(True, CTX_LEN)
    lat_ffn = _segment(False, FFN_TILE)
    lat_att = _segment(False, ATT_TQ)
    x_lat, x_ctx = x, ctx
    for layer in range(DEPTH):
        with_ctx = layer < DEPTH - 1
        row = lambda v: v[layer][None, :]
        tile2 = lambda v: jnp.tile(v[layer], LANES // HEAD_DIM)[None, :]
        proj_args = (mod, layer, row(attn_pre_norm), _layout_w_in(w_in[layer]), cos, sin_up, sin_dn,
                     tile2(gqa_q_norm), tile2(gqa_k_norm))
        proj = _in_proj(x_lat, lat_proj, None, *proj_args)
        gq, gk, gv, gr, gz, bq, bk, bv, cq, ck, cv = _in_proj(x_ctx, ctx_seg, proj, *proj_args)
        o_f, o_b = _gla(gq, gk, gv, gz, _layout_gate(gla_w_gate[layer]),
                        gla_b_gate[layer][:, None, :], tri, wst)
        b_att = _swa(swa_sink[layer].reshape(KV_HEADS, GROUP), bq, bk, bv, with_ctx)
        c_att = _gqa(cq, ck, cv, lat_att, TOK if with_ctx else SEQ, None)
        if with_ctx:
            c_att = _gqa(cq, ck, cv, ctx_seg, TOK, c_att)
        out_args = (o_f, o_b, gr, b_att, c_att, mod, layer, _layout_w_out(w_out[layer]),
                    tile2(gla_out_norm), row(attn_post_norm), row(ffn_pre_norm))
        ffn_args = (mod, layer, ffn_w_up[layer].astype(BF16), ffn_conv_w[layer], row(ffn_conv_b),
                    ffn_w_down[layer].astype(BF16), row(ffn_post_norm))
        x1, h2 = _out_proj(x_lat, lat_proj, *out_args)
        x_lat = _ffn(h2, x1, lat_ffn, *ffn_args)
        if with_ctx:
            x1, h2 = _out_proj(x_ctx, ctx_seg, *out_args)
            x_ctx = _ffn(h2, x1, ctx_seg, *ffn_args)
    return x_lat
```

```python
import functools
import itertools
from typing import NamedTuple

import jax
import jax.numpy as jnp
from jax import lax
from jax.experimental import pallas as pl
from jax.experimental.pallas import tpu as pltpu

F32 = jnp.float32
BF16 = jnp.bfloat16

D_MODEL = 1024
BATCH = 8
SEQ = 2048
DEPTH = 2
CTX_LEN = 256
TOK = SEQ + CTX_LEN
GRID_W = 64
HEAD_DIM = 64
ROPE_THETA = 10000.0
NORM_EPS = 1e-6
MOD_CHUNKS = 6

GLA_HEADS = 4
GLA_DK = 32
GLA_DV = 64
GLA_GATE_RANK = 16
GLA_GATE_TAU = 16.0
GLA_QK = GLA_HEADS * GLA_DK
GLA_WIDTH = GLA_HEADS * GLA_DV

Q_HEADS = 6
KV_HEADS = 2
GROUP = Q_HEADS // KV_HEADS
Q_WIDTH = Q_HEADS * HEAD_DIM
KV_WIDTH = KV_HEADS * HEAD_DIM
SWA_BLOCK = 128
SWA_TILE = 512

FFN_DIM = 2816
FFN_CHUNK = 256

LANES = 128
PROJ_TILE = 1024
ROW_PART = 256
FFN_TILE = 1024
FFN_PART = 256
HALO = 16
GLA_BLOCK = 256
GLA_CHUNK = 64
GLA_SUB = 16
GLA_BATCH = 4
ATT_TQ = 1024
ATT_ROWS = 512
VMEM_LIMIT = 56 * 1024 * 1024

C_GQ, C_GK, C_GV, C_GR = 0, 128, 256, 512
C_BQ, C_CQ = 768, 1152
C_BK, C_BV, C_CK, C_CV = 1536, 1664, 1792, 1920
C_GZ = 2048
IN_COLS = 2176
HEAD_PERM = (0, 3, 1, 4, 2, 5)

_NT = (((1,), (1,)), ((), ()))


class _Segment(NamedTuple):
    tile: int
    n_tiles: int
    first: int
    ctx: bool


def _segment(ctx, tile):
    if ctx:
        return _Segment(CTX_LEN, 1, SEQ // CTX_LEN, True)
    return _Segment(tile, SEQ // tile, 0, False)


def _dot(a, b):
    return jnp.dot(a, b, preferred_element_type=F32)


def _dot_nt(a, b):
    return lax.dot_general(a, b, _NT, preferred_element_type=F32)


def _idiv(x, n):
    assert n & (n - 1) == 0
    return lax.shift_right_logical(x, jnp.int32(n.bit_length() - 1))


def _split2(x):
    hi = x.astype(BF16)
    lo = (x - hi.astype(F32)).astype(BF16)
    return hi, lo


def _split3(x):
    h1 = x.astype(BF16)
    r1 = x - h1.astype(F32)
    h2 = r1.astype(BF16)
    h3 = (r1 - h2.astype(F32)).astype(BF16)
    return h1, h2, h3


def _seg_ones(width, seg):
    r = _idiv(lax.broadcasted_iota(jnp.int32, (width, width), 0), seg)
    c = _idiv(lax.broadcasted_iota(jnp.int32, (width, width), 1), seg)
    return jnp.where(r == c, 1.0, 0.0).astype(BF16)


def _seg_sum(x, ones):
    hi, lo = _split2(x)
    return _dot(hi, ones) + _dot(lo, ones)


def _row_rms(x, w):
    ms = jnp.mean(x * x, axis=-1, keepdims=True)
    return x * lax.rsqrt(ms + NORM_EPS) * w


def _head_rms(x, ones, w):
    ss = _seg_sum(x * x, ones)
    return x * lax.rsqrt(ss * (1.0 / HEAD_DIM) + NORM_EPS) * w


def _silu(x):
    return x / (1.0 + jnp.exp(-x))


def _rope(x, cos, sin_up, sin_dn):
    return x * cos + pltpu.roll(x, LANES - 16, 1) * sin_up + pltpu.roll(x, 16, 1) * sin_dn


def _const_spec(shape):
    return pl.BlockSpec(shape, lambda b, i: (0,) * len(shape))


def _params(*semantics):
    return pltpu.CompilerParams(dimension_semantics=semantics, vmem_limit_bytes=VMEM_LIMIT)


def _mod_kernel(c_ref, w_ref, b_ref, o_ref):
    a = _silu(c_ref[...]).astype(BF16)
    o_ref[0] = _dot(a, w_ref[0].astype(BF16)) + b_ref[0]


def _modulation(cs, w_mod, b_mod):
    tn = 1536
    n = MOD_CHUNKS * D_MODEL
    return pl.pallas_call(
        _mod_kernel,
        grid=(DEPTH, n // tn),
        in_specs=[
            pl.BlockSpec((16, D_MODEL), lambda l, j: (0, 0)),
            pl.BlockSpec((1, D_MODEL, tn), lambda l, j: (l, 0, j)),
            pl.BlockSpec((1, 1, tn), lambda l, j: (l, 0, j)),
        ],
        out_specs=pl.BlockSpec((1, 16, tn), lambda l, j: (l, 0, j)),
        out_shape=jax.ShapeDtypeStruct((DEPTH, 16, n), F32),
        compiler_params=_params("arbitrary", "arbitrary"),
        name="modulation",
    )(cs, w_mod, b_mod.reshape(DEPTH, 1, n))


def _mod_spec(layer, seg):
    return pl.BlockSpec((1, 1, MOD_CHUNKS, D_MODEL),
                        lambda b, i: (layer, BATCH if seg.ctx else b, 0, 0))


_PROJ_OUTPUTS = ((GLA_QK, F32), (GLA_QK, F32), (GLA_WIDTH, F32), (GLA_WIDTH, F32), (LANES, F32),
                 (Q_WIDTH, BF16), (KV_WIDTH, BF16), (KV_WIDTH, BF16),
                 (Q_WIDTH, BF16), (KV_WIDTH, BF16), (KV_WIDTH, BF16))


def _interleave(streams):
    for _ in itertools.zip_longest(*streams):
        pass


def _lag(stream, stages):
    for _ in range(stages):
        yield
    yield from stream


def _in_proj_kernel(x_ref, mod_ref, nw_ref, w_ref, cos_ref, su_ref, sd_ref, qn_ref, kn_ref, *rest):
    (gq_ref, gk_ref, gv_ref, gr_ref, gz_ref, bq_ref, bk_ref, bv_ref,
     cq_ref, ck_ref, cv_ref) = rest[-len(_PROJ_OUTPUTS):]
    sh = mod_ref[0, 0, 0:1, :]
    sc = mod_ref[0, 0, 1:2, :]
    ones = _seg_ones(LANES, HEAD_DIM)
    att_scale = HEAD_DIM ** -0.5

    def part(rows):
        h = (_row_rms(x_ref[0, rows, :], nw_ref[...]) * (1.0 + sc) + sh).astype(BF16)

        def proj(c0, width):
            return _dot(h, w_ref[:, c0:c0 + width])

        cos, su, sd = cos_ref[rows, :], su_ref[rows, :], sd_ref[rows, :]
        p_a = proj(C_GQ, C_BQ - C_GQ)
        p_q = proj(C_BQ, C_BK - C_BQ)
        yield
        gq_ref[0, rows, :] = p_a[:, C_GQ:C_GQ + GLA_QK] * (GLA_DK ** -0.5)
        gk_ref[0, rows, :] = p_a[:, C_GK:C_GK + GLA_QK]
        gv_ref[0, rows, :] = p_a[:, C_GV:C_GV + GLA_WIDTH]
        gr_ref[0, rows, :] = p_a[:, C_GR:C_GR + GLA_WIDTH]
        p_kv = proj(C_BK, IN_COLS - C_BK)

        def kv(c0):
            return p_kv[:, c0 - C_BK:c0 - C_BK + KV_WIDTH]
        yield
        for g in range(GROUP):
            ls = slice(g * LANES, (g + 1) * LANES)
            bq_ref[0, rows, ls] = (_rope(p_q[:, ls], cos, su, sd) * att_scale).astype(BF16)
            cq = _head_rms(p_q[:, Q_WIDTH + g * LANES:Q_WIDTH + (g + 1) * LANES], ones, qn_ref[...])
            cq_ref[0, rows, ls] = (_rope(cq, cos, su, sd) * att_scale).astype(BF16)
        yield
        bk_ref[0, rows, :] = _rope(kv(C_BK), cos, su, sd).astype(BF16)
        bv_ref[0, rows, :] = kv(C_BV).astype(BF16)
        ck_ref[0, rows, :] = _rope(_head_rms(kv(C_CK), ones, kn_ref[...]), cos, su, sd).astype(BF16)
        cv_ref[0, rows, :] = kv(C_CV).astype(BF16)
        gz_ref[0, rows, :] = kv(C_GZ)
        yield

    tile = x_ref.shape[1]
    _interleave([part(slice(r, r + ROW_PART)) for r in range(0, tile, ROW_PART)])


def _in_proj(x_seg, seg, into, mod, layer, nw, w, cos, su, sd, qn, kn):
    tile, first = seg.tile, seg.first

    def rows(width):
        return pl.BlockSpec((1, tile, width), lambda b, i: (b, first + i, 0))

    def table():
        return pl.BlockSpec((tile, LANES), lambda b, i: (first + i, 0))

    in_specs = [pl.BlockSpec((1, tile, D_MODEL), lambda b, i: (b, i, 0)), _mod_spec(layer, seg),
                _const_spec((1, D_MODEL)), _const_spec((D_MODEL, IN_COLS)),
                table(), table(), table(), _const_spec((1, LANES)), _const_spec((1, LANES))]
    args = [x_seg, mod, nw, w, cos, su, sd, qn, kn]
    aliases = {}
    if into is not None:
        aliases = {len(args) + k: k for k in range(len(into))}
        in_specs += [pl.BlockSpec(memory_space=pl.ANY)] * len(into)
        args += list(into)
    return pl.pallas_call(
        _in_proj_kernel,
        grid=(BATCH, seg.n_tiles),
        in_specs=in_specs,
        out_specs=[rows(wd) for wd, _ in _PROJ_OUTPUTS],
        out_shape=[jax.ShapeDtypeStruct((BATCH, TOK, wd), dt) for wd, dt in _PROJ_OUTPUTS],
        input_output_aliases=aliases,
        compiler_params=_params("parallel", "parallel"),
        name="in_proj_ctx" if seg.ctx else "in_proj",
    )(*args)


def _lane_group_mask(width, group, h):
    lane = lax.broadcasted_iota(jnp.int32, (1, width), 1)
    return _idiv(lane, group) == h


def _gla_direction(q_ref, k_ref, v_ref, z_ref, wg, bg, tri, wst_ref, o_ref, st_ref, b_s, e_s, rev):
    nb, c_len, sb = GLA_BLOCK, GLA_CHUNK, GLA_SUB
    n_chunks, n_sub = nb // c_len, c_len // sb

    g = _dot(z_ref[0].astype(BF16), wg) + bg
    la = (jnp.minimum(g, 0.0) - jnp.log1p(jnp.exp(-jnp.abs(g)))) * (1.0 / GLA_GATE_TAU)

    h1, h2, h3 = _split3(la)
    b = _dot(tri, h1) + _dot(tri, h2) + _dot(tri, h3)
    b_s[...] = b
    yield

    q, k, v = q_ref[0], k_ref[0], v_ref[0]

    half = sb // 2
    t_half = lax.broadcasted_iota(jnp.int32, (half, 1), 0)

    for jb in range(nb // sb):
        r0 = jb * sb
        bb = [b_s[pl.ds(r0 + lo, half), :] for lo in (0, half)]
        qb = [q_ref[0, pl.ds(r0 + lo, half), :] for lo in (0, half)]
        for s in range(sb):
            bs = b_s[pl.ds(r0 + s, 1), :]
            ks = k_ref[0, pl.ds(r0 + s, 1), :]
            halves = []
            for hi, lo in enumerate((0, half)):
                none = (lo > s) if rev else (lo + half - 1 < s)
                every = (lo + half - 1 <= s) if rev else (lo >= s)
                if none:
                    halves.append(jnp.zeros((half, LANES), F32))
                    continue
                diff = bb[hi] - bs
                if every:
                    e = qb[hi] * ks * jnp.exp(diff)
                else:
                    keep = (t_half + lo <= s) if rev else (t_half + lo >= s)
                    e = jnp.where(keep, qb[hi] * ks * jnp.exp(jnp.minimum(diff, 0.0)), 0.0)
                halves.append(e)
            e_s[pl.ds(r0, sb), s * LANES:(s + 1) * LANES] = jnp.concatenate(
                halves, axis=0).astype(BF16)
        if jb % 4 == 3:
            yield

    a_diag = _dot(e_s[...], wst_ref[...])
    yield

    hm_k = [_lane_group_mask(GLA_QK, GLA_DK, h) for h in range(GLA_HEADS)]
    hm_v = [_lane_group_mask(GLA_WIDTH, GLA_DV, h) for h in range(GLA_HEADS)]
    st_row = _idiv(lax.broadcasted_iota(jnp.int32, (GLA_WIDTH, GLA_QK), 0), GLA_DV)
    st_col = _idiv(lax.broadcasted_iota(jnp.int32, (GLA_WIDTH, GLA_QK), 1), GLA_DK)
    st_mask = st_row == st_col
    v_t = jnp.transpose(v).astype(BF16)
    chunk = [slice(c * c_len, (c + 1) * c_len) for c in range(n_chunks)]
    sub = [slice(j * sb, (j + 1) * sb) for j in range(n_sub)]

    q_hat, kv_t, d_row = [], [], []
    for c in range(n_chunks):
        bc = b[chunk[c]]
        end = 0 if rev else c_len - 1
        b_end = bc[end:end + 1]
        q_hat.append((q[chunk[c]] * jnp.exp(bc)).astype(BF16))
        d_row.append(jnp.exp(b_end))
        k_hat = (k[chunk[c]] * jnp.exp(b_end - bc)).astype(BF16)
        pads = [jnp.zeros((n, GLA_QK), BF16) for n in (c * c_len, nb - (c + 1) * c_len) if n]
        k_hat = jnp.concatenate(pads[:1 if c else 0] + [k_hat] + pads[1 if c else 0:], axis=0)
        kv_t.append(jnp.where(st_mask, _dot(v_t, k_hat), 0.0))
    yield

    scores = {}
    for c in range(n_chunks):
        bc, qc, kc = b[chunk[c]], q[chunk[c]], k[chunk[c]]
        for j in range(n_sub):
            if not ((j > 0) if rev else (j < n_sub - 1)):
                continue
            edge = j * sb if rev else (j + 1) * sb - 1
            ref_b = bc[edge:edge + 1]
            later = slice(0, j * sb) if rev else slice((j + 1) * sb, c_len)
            q_l = (qc[later] * jnp.exp(bc[later] - ref_b)).astype(BF16)
            k_j = kc[sub[j]] * jnp.exp(ref_b - bc[sub[j]])
            k_tile = jnp.concatenate(
                [jnp.where(hm_k[h], k_j, 0.0) for h in range(GLA_HEADS)], axis=0).astype(BF16)
            scores[(c, j)] = _dot_nt(q_l, k_tile)
        yield

    o_intra = []
    for c in range(n_chunks):
        vc = v[chunk[c]]
        blocks = [None] * n_sub
        for j in range(n_sub):
            v_exp = jnp.concatenate(
                [jnp.where(hm_v[h], vc[sub[j]], 0.0) for h in range(GLA_HEADS)],
                axis=0).astype(BF16)
            ad = a_diag[c * c_len + j * sb:c * c_len + (j + 1) * sb]
            if (c, j) in scores:
                sc = scores[(c, j)]
                p = jnp.concatenate([sc, ad], axis=0) if rev else jnp.concatenate([ad, sc], axis=0)
                first = 0 if rev else j
            else:
                p, first = ad, j
            contrib = _dot(p.astype(BF16), v_exp)
            for r in range(contrib.shape[0] // sb):
                piece = contrib[r * sb:(r + 1) * sb]
                blocks[first + r] = piece if blocks[first + r] is None else blocks[first + r] + piece
        o_intra.append(jnp.concatenate(blocks, axis=0))
        yield

    state = st_ref[...]
    outs = [None] * n_chunks
    for c in (reversed(range(n_chunks)) if rev else range(n_chunks)):
        outs[c] = o_intra[c] + _dot_nt(q_hat[c], state.astype(BF16))
        state = state * d_row[c] + kv_t[c]
        yield
    st_ref[...] = state
    o_ref[0, 0] = jnp.concatenate(outs, axis=0)


def _gla_kernel(qf, kf, vf, zf, qb, kb, vb, zb, wg_ref, bg_ref, tri_ref, wst_ref, of_ref, ob_ref,
                sf_ref, sb_ref, bf_s, bb_s, ef_s, eb_s):
    @pl.when(pl.program_id(1) == 0)
    def _():
        sf_ref[...] = jnp.zeros_like(sf_ref)
        sb_ref[...] = jnp.zeros_like(sb_ref)

    streams = []
    for bi in range(GLA_BATCH):
        one = pl.ds(bi, 1)
        streams.append(_gla_direction(
            qf.at[one], kf.at[one], vf.at[one], zf.at[one], wg_ref[0], bg_ref[0], tri_ref[0],
            wst_ref, of_ref.at[:, one], sf_ref.at[bi], bf_s.at[bi], ef_s.at[bi], False))
        streams.append(_gla_direction(
            qb.at[one], kb.at[one], vb.at[one], zb.at[one], wg_ref[1], bg_ref[1], tri_ref[1],
            wst_ref, ob_ref.at[:, one], sb_ref.at[bi], bb_s.at[bi], eb_s.at[bi], True))
    _interleave([_lag(s, n) for n, s in enumerate(streams)])


def _cumsum_matrices():
    t = jnp.arange(GLA_BLOCK)[:, None]
    s = jnp.arange(GLA_BLOCK)[None, :]
    inside = (t // GLA_CHUNK) == (s // GLA_CHUNK)
    return jnp.stack([jnp.where(inside & (s <= t), 1.0, 0.0),
                      jnp.where(inside & (s >= t), 1.0, 0.0)]).astype(BF16)


def _gla(gq, gk, gv, gz, wg, bg, tri, wst):
    n_lat = SEQ // GLA_BLOCK
    ctx_blk = n_lat

    def fwd(b, s):
        return (b, jnp.where(s == 0, ctx_blk, s - 1), 0)

    def bwd(b, s):
        return (b, jnp.where(s == 0, ctx_blk, n_lat - s), 0)

    def specs(index_map):
        return [pl.BlockSpec((GLA_BATCH, GLA_BLOCK, wd), index_map)
                for wd in (GLA_QK, GLA_QK, GLA_WIDTH, LANES)]

    return pl.pallas_call(
        _gla_kernel,
        grid=(BATCH // GLA_BATCH, n_lat + 1),
        in_specs=specs(fwd) + specs(bwd) + [
            _const_spec((2, LANES, GLA_QK)), _const_spec((2, 1, GLA_QK)),
            _const_spec((2, GLA_BLOCK, GLA_BLOCK)),
            _const_spec((GLA_SUB * LANES, GLA_HEADS * GLA_SUB))],
        out_specs=[pl.BlockSpec((1, GLA_BATCH, GLA_BLOCK, GLA_WIDTH), lambda b, s: (0,) + fwd(b, s)),
                   pl.BlockSpec((1, GLA_BATCH, GLA_BLOCK, GLA_WIDTH), lambda b, s: (0,) + bwd(b, s))],
        out_shape=[jax.ShapeDtypeStruct((1, BATCH, TOK, GLA_WIDTH), F32)] * 2,
        scratch_shapes=[pltpu.VMEM((GLA_BATCH, GLA_WIDTH, GLA_QK), F32)] * 2
        + [pltpu.VMEM((GLA_BATCH, GLA_BLOCK, GLA_QK), F32)] * 2
        + [pltpu.VMEM((GLA_BATCH, GLA_BLOCK, GLA_SUB * LANES), BF16)] * 2,
        compiler_params=_params("parallel", "arbitrary"),
        name="gla",
    )(gq, gk, gv, gz, gq, gk, gv, gz, wg, bg, tri, wst)


def _fold_lanes(x, op):
    acc = x[:, 0:LANES]
    for c in range(1, x.shape[1] // LANES):
        acc = op(acc, x[:, c * LANES:(c + 1) * LANES])
    return acc


class _Job(NamedTuple):
    rows: slice
    keys: jax.Array
    values: jax.Array
    bias: jax.Array


def _attend(q_ref, jobs, sink_ref, groups, ahead):
    lane = lax.broadcasted_iota(jnp.int32, (1, LANES), 1)
    first = lane < HEAD_DIM
    units = [(jb, h, gs) for jb in range(len(jobs)) for gs in groups for h in range(KV_HEADS)]

    def scores(jb, h, gs):
        mine = first if h == 0 else jnp.logical_not(first)
        qz = [jnp.where(mine, q_ref[0, jobs[jb].rows, g * LANES:(g + 1) * LANES], 0) for g in gs]
        qz = qz[0] if len(gs) == 1 else jnp.concatenate(qz, axis=0)
        return _dot_nt(qz, jobs[jb].keys)

    pending = {}
    for idx in range(min(ahead, len(units))):
        pending[idx] = scores(*units[idx])
    outs = {}
    for idx, (jb, h, gs) in enumerate(units):
        if idx + ahead < len(units):
            pending[idx + ahead] = scores(*units[idx + ahead])
        s = pending.pop(idx)
        job = jobs[jb]
        tq = job.rows.stop - job.rows.start
        if job.bias is not None:
            nb = job.bias.shape[1]
            s = jnp.concatenate([s[:, :nb] + job.bias[:s.shape[0]], s[:, nb:]], axis=1)
        m = jnp.max(_fold_lanes(s, jnp.maximum), axis=-1, keepdims=True)
        if sink_ref is not None:
            sk = [jnp.full((tq, 1), sink_ref[h, g], F32) for g in gs]
            sk = sk[0] if len(gs) == 1 else jnp.concatenate(sk, axis=0)
            m = jnp.maximum(m, sk)
        p = jnp.exp(s - m)
        den = jnp.sum(_fold_lanes(p, jnp.add), axis=-1, keepdims=True)
        if sink_ref is not None:
            den = den + jnp.exp(sk - m)
        o = _dot(p.astype(BF16), job.values) / den
        for r, g in enumerate(gs):
            outs[(jb, h, g)] = o[r * tq:(r + 1) * tq]
    return [jnp.concatenate([jnp.where(first, outs[(jb, 0, g)], outs[(jb, 1, g)])
                             for g in range(GROUP)], axis=1) for jb in range(len(jobs))]


_ALL_GROUPS = (tuple(range(GROUP)),)
_EACH_GROUP = tuple((g,) for g in range(GROUP))


def _swa_ctx_kernel(sink_ref, q_ref, kc_ref, vc_ref, _, o_ref):
    job = _Job(slice(0, q_ref.shape[1]), kc_ref[0], vc_ref[0], None)
    o_ref[0] = _attend(q_ref, [job], sink_ref, _ALL_GROUPS, 2)[0].astype(o_ref.dtype)


def _swa_kernel(sink_ref, band_ref, q_ref, kc_ref, vc_ref, kp_ref, kk_ref, kn_ref, vp_ref, vk_ref,
                vn_ref, o_ref, *, n_steps):
    i = pl.program_id(1)
    w = SWA_BLOCK
    tile = q_ref.shape[1]
    n_local = tile + 2 * w
    keys = jnp.concatenate([kp_ref[0], kk_ref[0], kn_ref[0], kc_ref[0]], axis=0)
    values = jnp.concatenate([vp_ref[0], vk_ref[0], vn_ref[0], vc_ref[0]], axis=0)
    col = lax.broadcasted_iota(jnp.int32, (1, n_local), 1)
    gone = ((col < w) & (i == 0)) | ((col >= n_local - w) & (i == n_steps - 1))
    bias = band_ref[...] + jnp.where(gone, -1e30, 0.0)
    job = _Job(slice(0, tile), keys, values, bias)
    o_ref[0] = _attend(q_ref, [job], sink_ref, _EACH_GROUP, 1)[0].astype(o_ref.dtype)


def _swa(sink, bq, bk, bv, with_ctx):
    w = SWA_BLOCK
    n_blocks = SEQ // w
    per = SWA_TILE // w
    n_steps = SEQ // SWA_TILE
    rows = TOK if with_ctx else SEQ

    def edge_block(which):
        return pl.BlockSpec(
            (1, w, KV_WIDTH),
            lambda b, i: (b, jnp.clip(per * i - 1 if which < 0 else per * (i + 1), 0, n_blocks - 1), 0))

    own_spec = pl.BlockSpec((1, SWA_TILE, KV_WIDTH), lambda b, i: (b, i, 0))
    q_spec = pl.BlockSpec((1, SWA_TILE, Q_WIDTH), lambda b, i: (b, i, 0))
    ctx_spec = pl.BlockSpec((1, CTX_LEN, KV_WIDTH), lambda b, i: (b, SEQ // CTX_LEN, 0))
    sink_spec = pl.BlockSpec(memory_space=pltpu.SMEM)

    t = jnp.arange(SWA_TILE)[:, None]
    kk = jnp.arange(SWA_TILE + 2 * w)[None, :]
    band = jnp.where((kk - t >= 0) & (kk - t <= 2 * w), 0.0, -1e30).astype(F32)

    out = pl.pallas_call(
        functools.partial(_swa_kernel, n_steps=n_steps),
        grid=(BATCH, n_steps),
        in_specs=[sink_spec, _const_spec((SWA_TILE, SWA_TILE + 2 * w)), q_spec, ctx_spec, ctx_spec,
                  edge_block(-1), own_spec, edge_block(1),
                  edge_block(-1), own_spec, edge_block(1)],
        out_specs=q_spec,
        out_shape=jax.ShapeDtypeStruct((BATCH, rows, Q_WIDTH), BF16),
        compiler_params=_params("parallel", "parallel"),
        name="swa",
    )(sink, band, bq, bk, bv, bk, bk, bk, bv, bv, bv)
    if not with_ctx:
        return out
    ctx_q_spec = pl.BlockSpec((1, CTX_LEN, Q_WIDTH), lambda b, i: (b, SEQ // CTX_LEN, 0))
    return pl.pallas_call(
        _swa_ctx_kernel,
        grid=(BATCH, 1),
        in_specs=[sink_spec, ctx_q_spec, ctx_spec, ctx_spec, pl.BlockSpec(memory_space=pl.ANY)],
        out_specs=ctx_q_spec,
        out_shape=jax.ShapeDtypeStruct((BATCH, rows, Q_WIDTH), BF16),
        input_output_aliases={4: 0},
        compiler_params=_params("parallel", "parallel"),
        name="swa_ctx",
    )(sink, bq, bk, bv, out)


def _gqa_kernel(q_ref, k_ref, v_ref, *rest, ctx):
    o_ref = rest[-1]
    if ctx:
        keys, values = k_ref[0, SEQ:TOK, :], v_ref[0, SEQ:TOK, :]
    else:
        keys, values = k_ref[0], v_ref[0]
    tq = q_ref.shape[1]
    rows = min(tq, ATT_ROWS)
    jobs = [_Job(slice(r, r + rows), keys, values, None) for r in range(0, tq, rows)]
    outs = _attend(q_ref, jobs, None, _EACH_GROUP, 1)
    o_ref[0] = (outs[0] if len(outs) == 1 else jnp.concatenate(outs, axis=0)).astype(o_ref.dtype)


def _gqa(cq, ck, cv, seg, rows, into):
    tile, first = seg.tile, seg.first
    kv_spec = pl.BlockSpec((1, TOK, KV_WIDTH), lambda b, i: (b, 0, 0))
    q_spec = pl.BlockSpec((1, tile, Q_WIDTH), lambda b, i: (b, first + i, 0))
    in_specs, args, aliases = [q_spec, kv_spec, kv_spec], [cq, ck, cv], {}
    if into is not None:
        in_specs.append(pl.BlockSpec(memory_space=pl.ANY))
        args.append(into)
        aliases = {3: 0}
    return pl.pallas_call(
        functools.partial(_gqa_kernel, ctx=seg.ctx),
        grid=(BATCH, seg.n_tiles),
        in_specs=in_specs,
        out_specs=q_spec,
        out_shape=jax.ShapeDtypeStruct((BATCH, rows, Q_WIDTH), BF16),
        input_output_aliases=aliases,
        compiler_params=_params("parallel", "parallel"),
        name="gqa_ctx" if seg.ctx else "gqa",
    )(*args)


def _out_proj_kernel(x_ref, of_ref, ob_ref, r_ref, b_ref, c_ref, mod_ref, w_ref, on_ref,
                     post_ref, pre_ref, x1_ref, h2_ref):
    ones = _seg_ones(LANES, GLA_DV)
    g1 = mod_ref[0, 0, 2:3, :]
    sh2 = mod_ref[0, 0, 3:4, :]
    sc2 = mod_ref[0, 0, 4:5, :]

    def part(rows):
        mix = _dot(b_ref[0, rows, :], w_ref[GLA_WIDTH:GLA_WIDTH + Q_WIDTH, :])
        mix = mix + _dot(c_ref[0, rows, :], w_ref[GLA_WIDTH + Q_WIDTH:, :])
        ya = []
        for lp in range(GLA_WIDTH // LANES):
            ls = slice(lp * LANES, (lp + 1) * LANES)
            o = of_ref[0, 0, rows, ls] + ob_ref[0, 0, rows, ls]
            ya.append((_head_rms(o, ones, on_ref[...]) * _silu(r_ref[0, rows, ls])).astype(BF16))
        mix = mix + _dot(jnp.concatenate(ya, axis=1), w_ref[0:GLA_WIDTH, :])
        yield
        x1 = x_ref[0, rows, :] + g1 * _row_rms(mix, post_ref[...])
        x1_ref[0, rows, :] = x1
        h2_ref[0, rows, :] = (_row_rms(x1, pre_ref[...]) * (1.0 + sc2) + sh2).astype(BF16)
        yield

    tile = x_ref.shape[1]
    _interleave([part(slice(r, r + ROW_PART)) for r in range(0, tile, ROW_PART)])


def _out_proj(x_seg, seg, o_f, o_b, gr, b_att, c_att, mod, layer, w, on, post, pre):
    tile, first = seg.tile, seg.first

    def own(width):
        return pl.BlockSpec((1, tile, width), lambda b, i: (b, i, 0))

    def combined(width):
        return pl.BlockSpec((1, tile, width), lambda b, i: (b, first + i, 0))

    gla_spec = pl.BlockSpec((1, 1, tile, GLA_WIDTH), lambda b, i: (0, b, first + i, 0))
    rows = seg.n_tiles * tile
    return pl.pallas_call(
        _out_proj_kernel,
        grid=(BATCH, seg.n_tiles),
        in_specs=[own(D_MODEL), gla_spec, gla_spec, combined(GLA_WIDTH),
                  combined(Q_WIDTH), combined(Q_WIDTH), _mod_spec(layer, seg),
                  _const_spec((D_MODEL, D_MODEL)), _const_spec((1, LANES)),
                  _const_spec((1, D_MODEL)), _const_spec((1, D_MODEL))],
        out_specs=[own(D_MODEL), own(D_MODEL)],
        out_shape=[jax.ShapeDtypeStruct((BATCH, rows, D_MODEL), F32),
                   jax.ShapeDtypeStruct((BATCH, rows, D_MODEL), BF16)],
        compiler_params=_params("parallel", "parallel"),
        name="out_proj_ctx" if seg.ctx else "out_proj",
    )(x_seg, o_f, o_b, gr, b_att, c_att, mod, w, on, post, pre)


def _ffn_kernel(hp_ref, h_ref, hn_ref, x1_ref, mod_ref, wu_ref, cw_ref, cb_ref, wd_ref, post_ref,
                o_ref, *, n_tiles):
    i = pl.program_id(1)
    tile = h_ref.shape[1]
    hp = jnp.where(i != 0, hp_ref[0], 0)
    hn = jnp.where(i != n_tiles - 1, hn_ref[0], 0)
    h_ext = jnp.concatenate([hp, h_ref[0], hn], axis=0)
    part = min(tile, FFN_PART)
    win = part + 2 * HALO
    mid = slice(HALO, HALO + part)

    def conv(u, c0):
        cw = cw_ref[:, c0:c0 + FFN_CHUNK]
        prev = pltpu.roll(u, 1, 0)[mid]
        nxt = pltpu.roll(u, win - 1, 0)[mid]
        return cw[0:1] * prev + cw[1:2] * u[mid] + cw[2:3] * nxt + cb_ref[:, c0:c0 + FFN_CHUNK]

    n_fc = FFN_DIM // FFN_CHUNK

    n_parts = tile // part
    cuts = [0] + [HALO + r * part for r in range(1, n_parts)] + [tile + 2 * HALO]
    lhs = [h_ext[cuts[r]:cuts[r + 1]] for r in range(n_parts)]

    def window(us, r):
        lo, hi = r * part, r * part + win
        pieces = []
        for q in range(n_parts):
            a, b = max(lo, cuts[q]), min(hi, cuts[q + 1])
            if a < b:
                pieces.append(us[q][a - cuts[q]:b - cuts[q]])
        return pieces[0] if len(pieces) == 1 else jnp.concatenate(pieces, axis=0)

    def up(fc):
        ca = fc * FFN_CHUNK
        cg = FFN_DIM + ca
        return ([_dot(x, wu_ref[:, ca:ca + FFN_CHUNK]) for x in lhs],
                [_dot(x, wu_ref[:, cg:cg + FFN_CHUNK]) for x in lhs])

    acc = [jnp.zeros((part, D_MODEL), F32) for _ in range(n_parts)]
    ua, ug = up(0)
    for fc in range(n_fc):
        nxt_u = up(fc + 1) if fc + 1 < n_fc else None
        ca = fc * FFN_CHUNK
        for r in range(n_parts):
            act = (_silu(conv(window(ua, r), ca)) * conv(window(ug, r), FFN_DIM + ca)).astype(BF16)
            acc[r] = acc[r] + _dot(act, wd_ref[ca:ca + FFN_CHUNK, :])
        if nxt_u is not None:
            ua, ug = nxt_u
    g2 = mod_ref[0, 0, 5:6, :]
    for r in range(tile // part):
        rows = slice(r * part, (r + 1) * part)
        o_ref[0, rows, :] = x1_ref[0, rows, :] + g2 * _row_rms(acc[r], post_ref[...])


def _ffn(h2, x1, seg, mod, layer, wu, cw, cb, wd, post):
    tile, n_tiles = seg.tile, seg.n_tiles
    per = tile // HALO
    n_halo = n_tiles * per

    def own(width):
        return pl.BlockSpec((1, tile, width), lambda b, i: (b, i, 0))

    prev_spec = pl.BlockSpec((1, HALO, D_MODEL), lambda b, i: (b, jnp.maximum(i * per - 1, 0), 0))
    next_spec = pl.BlockSpec((1, HALO, D_MODEL),
                             lambda b, i: (b, jnp.minimum((i + 1) * per, n_halo - 1), 0))
    return pl.pallas_call(
        functools.partial(_ffn_kernel, n_tiles=n_tiles),
        grid=(BATCH, n_tiles),
        in_specs=[prev_spec, own(D_MODEL), next_spec, own(D_MODEL), _mod_spec(layer, seg),
                  _const_spec((D_MODEL, 2 * FFN_DIM)), _const_spec((3, 2 * FFN_DIM)),
                  _const_spec((1, 2 * FFN_DIM)), _const_spec((FFN_DIM, D_MODEL)),
                  _const_spec((1, D_MODEL))],
        out_specs=own(D_MODEL),
        out_shape=jax.ShapeDtypeStruct((BATCH, n_tiles * tile, D_MODEL), F32),
        compiler_params=pltpu.CompilerParams(
            dimension_semantics=("parallel", "parallel"), vmem_limit_bytes=VMEM_LIMIT,
            ),
        name="ffn_ctx" if seg.ctx else "ffn",
    )(h2, h2, h2, x1, mod, wu, cw, cb, wd, post)


def _rope_tables():
    rows = SEQ // GRID_W
    row = jnp.repeat(jnp.arange(rows), GRID_W).astype(F32)
    col = (jnp.arange(rows * GRID_W) % GRID_W).astype(F32)
    n_freq = HEAD_DIM // 4
    inv_freq = ROPE_THETA ** (-jnp.arange(n_freq, dtype=F32) / n_freq)
    ang_r = row[:, None] * inv_freq[None, :]
    ang_c = col[:, None] * inv_freq[None, :]
    ang = jnp.concatenate([ang_r, ang_r, ang_c, ang_c], axis=-1)
    cos = jnp.tile(jnp.cos(ang), (1, LANES // HEAD_DIM))
    sin = jnp.tile(jnp.sin(ang), (1, LANES // HEAD_DIM))
    upper = (jnp.arange(LANES) % 32) < 16
    sin_up = jnp.where(upper[None, :], -sin, 0.0)
    sin_dn = jnp.where(upper[None, :], 0.0, sin)
    pad = jnp.zeros((CTX_LEN, LANES), F32)
    return (jnp.concatenate([cos, jnp.ones((CTX_LEN, LANES), F32)], axis=0),
            jnp.concatenate([sin_up, pad], axis=0), jnp.concatenate([sin_dn, pad], axis=0))


def _permute_heads(w, axis):
    heads = [lax.slice_in_dim(w, h * HEAD_DIM, (h + 1) * HEAD_DIM, axis=axis) for h in HEAD_PERM]
    return jnp.concatenate(heads, axis=axis)


def _layout_w_in(w):
    sizes = (GLA_QK, GLA_QK, GLA_WIDTH, GLA_WIDTH, 2 * GLA_GATE_RANK,
             Q_WIDTH, KV_WIDTH, KV_WIDTH, Q_WIDTH, KV_WIDTH, KV_WIDTH)
    w = w.astype(BF16)
    parts, start = [], 0
    for size in sizes:
        parts.append(w[:, start:start + size])
        start += size
    a_q, a_k, a_v, a_r, a_z, b_q, b_k, b_v, c_q, c_k, c_v = parts
    z_pad = jnp.zeros((w.shape[0], LANES - 2 * GLA_GATE_RANK), w.dtype)
    return jnp.concatenate(
        [a_q, a_k, a_v, a_r, _permute_heads(b_q, 1), _permute_heads(c_q, 1), b_k, b_v, c_k, c_v,
         a_z, z_pad], axis=1)


def _layout_w_out(w):
    w = w.astype(BF16)
    a = w[:GLA_WIDTH]
    b = _permute_heads(w[GLA_WIDTH:GLA_WIDTH + Q_WIDTH], 0)
    c = _permute_heads(w[GLA_WIDTH + Q_WIDTH:], 0)
    return jnp.concatenate([a, b, c], axis=0)


def _layout_gate(w_gate):
    out = jnp.zeros((2, LANES, GLA_QK), w_gate.dtype)
    for d in range(2):
        out = out.at[d, d * GLA_GATE_RANK:(d + 1) * GLA_GATE_RANK].set(w_gate[d])
    return out.astype(BF16)


def _diag_reduce_matrix():
    r = jnp.arange(GLA_SUB * LANES)
    c = jnp.arange(GLA_HEADS * GLA_SUB)
    same_s = (r[:, None] // LANES) == (c[None, :] % GLA_SUB)
    same_h = ((r[:, None] % LANES) // GLA_DK) == (c[None, :] // GLA_SUB)
    return jnp.where(same_s & same_h, 1.0, 0.0).astype(BF16)


def kernel(x, c, ctx, c_ctx, w_mod, b_mod, attn_pre_norm, attn_post_norm, ffn_pre_norm,
           ffn_post_norm, w_in, gla_w_gate, gla_b_gate, gla_out_norm, swa_sink, gqa_q_norm,
           gqa_k_norm, w_out, ffn_w_up, ffn_conv_w, ffn_conv_b, ffn_w_down):
    assert x.shape == (BATCH, SEQ, D_MODEL) and ctx.shape == (BATCH, CTX_LEN, D_MODEL)
    cos, sin_up, sin_dn = _rope_tables()
    wst = _diag_reduce_matrix()
    tri = _cumsum_matrices()
    cs = jnp.concatenate([c, c_ctx[None, :], jnp.zeros((16 - BATCH - 1, D_MODEL), F32)], axis=0)
    mod = _modulation(cs, w_mod, b_mod).reshape(DEPTH, 16, MOD_CHUNKS, D_MODEL)

    lat_proj, ctx_seg = _segment(False, PROJ_TILE), _segment(True, CTX_LEN)
    lat_ffn = _segment(False, FFN_TILE)
    lat_att = _segment(False, ATT_TQ)
    x_lat, x_ctx = x, ctx
    for layer in range(DEPTH):
        with_ctx = layer < DEPTH - 1
        row = lambda v: v[layer][None, :]
        tile2 = lambda v: jnp.tile(v[layer], LANES // HEAD_DIM)[None, :]
        proj_args = (mod, layer, row(attn_pre_norm), _layout_w_in(w_in[layer]), cos, sin_up, sin_dn,
                     tile2(gqa_q_norm), tile2(gqa_k_norm))
        proj = _in_proj(x_lat, lat_proj, None, *proj_args)
        gq, gk, gv, gr, gz, bq, bk, bv, cq, ck, cv = _in_proj(x_ctx, ctx_seg, proj, *proj_args)
        o_f, o_b = _gla(gq, gk, gv, gz, _layout_gate(gla_w_gate[layer]),
                        gla_b_gate[layer][:, None, :], tri, wst)
        b_att = _swa(swa_sink[layer].reshape(KV_HEADS, GROUP), bq, bk, bv, with_ctx)
        c_att = _gqa(cq, ck, cv, lat_att, TOK if with_ctx else SEQ, None)
        if with_ctx:
            c_att = _gqa(cq, ck, cv, ctx_seg, TOK, c_att)
        out_args = (o_f, o_b, gr, b_att, c_att, mod, layer, _layout_w_out(w_out[layer]),
                    tile2(gla_out_norm), row(attn_post_norm), row(ffn_pre_norm))
        ffn_args = (mod, layer, ffn_w_up[layer].astype(BF16), ffn_conv_w[layer], row(ffn_conv_b),
                    ffn_w_down[layer].astype(BF16), row(ffn_post_norm))
        x1, h2 = _out_proj(x_lat, lat_proj, *out_args)
        x_lat = _ffn(h2, x1, lat_ffn, *ffn_args)
        if with_ctx:
            x1, h2 = _out_proj(x_ctx, ctx_seg, *out_args)
            x_ctx = _ffn(h2, x1, ctx_seg, *ffn_args)
    return x_lat
```
